```python
import math, functools
import jax, jax.numpy as jnp
from jax import lax
import numpy as np

D_MODEL = 1024
BATCH = 8
SEQ = 8192
DEPTH = 1
DEC_BATCH = 128
DEC_SEQ = 1
PAST_LEN = 8192
PAGE_SIZE = 128

ATT_HEADS = 8
ATT_DIM = 64
IDX_HEADS = 8
IDX_DIM = 64
TOPK_MAX = 256
Q_BLOCK = 128
ML_HEADS = 4
ML_QK = 64
ML_V = 128
ML_CHUNK = 64
GATE_CAP = 15.0
N_EXPERTS = 32
TOP_K = 4
D_FF = D_MODEL
SWIGLU_LIMIT = 7.0
SWIGLU_ALPHA = 1.702
MOE_BLOCK = 128
ROPE_THETA = 10000.0
EPS = 1e-5

ATT_W = ATT_HEADS * ATT_DIM
ML_W = ML_HEADS * ML_V
MIX_W = ATT_W + ML_W
SPLITS = (ATT_W, ATT_W, ATT_W, IDX_HEADS * IDX_DIM, IDX_HEADS, IDX_DIM,
          ML_HEADS * ML_QK, ML_HEADS * ML_QK, ML_W, ML_W, ML_HEADS, ML_HEADS)
IN_W = sum(SPLITS)
SPLIT_POINTS = tuple(int(s) for s in np.cumsum(SPLITS)[:-1])

kernel_name = 'hymba_dsa_mlstm_moe_step'

F32 = jnp.float32


def rmsnorm(x, g):
    xf = x.astype(F32)
    r = lax.rsqrt(jnp.mean(xf * xf, axis=-1, keepdims=True) + EPS)
    return (xf * r * g.astype(F32)).astype(x.dtype)


def rope(x, pos):
    half = x.shape[-1] // 2
    inv = ROPE_THETA ** (-jnp.arange(half, dtype=F32) / half)
    ang = pos.astype(F32)[:, None] * inv[None, :]
    cos = jnp.cos(ang)[:, None, :]
    sin = jnp.sin(ang)[:, None, :]
    xf = x.astype(F32)
    x1, x2 = xf[..., :half], xf[..., half:]
    return jnp.concatenate([x1 * cos - x2 * sin, x2 * cos + x1 * sin], axis=-1).astype(x.dtype)


def softcap(a):
    return GATE_CAP * jnp.tanh(a / GATE_CAP)


def index_scores(iq, iw, ik):
    s = jnp.einsum('bthd,bsd->bths', iq, ik).astype(F32)
    return jnp.einsum('bths,bth->bts', jax.nn.relu(s), iw.astype(F32))


def select_keys(scores, q_pos, topk):
    s_len = scores.shape[-1]
    admissible = jnp.arange(s_len)[None, None, :] <= q_pos[None, :, None]
    scores = jnp.where(admissible, scores, -jnp.inf)
    _, idx = lax.top_k(scores, topk)
    valid = idx <= q_pos[None, :, None]
    return idx, valid


def sparse_attend(q, k_sel, v_sel, valid):
    s = jnp.einsum('bthd,btkhd->bthk', q, k_sel).astype(F32) * (ATT_DIM ** -0.5)
    s = jnp.where(valid[:, :, None, :], s, -jnp.inf)
    p = jax.nn.softmax(s, axis=-1)
    o = jnp.einsum('bthk,btkhd->bthd', p.astype(v_sel.dtype), v_sel)
    return o.reshape(q.shape[0], q.shape[1], ATT_W)


_gather_rows = jax.vmap(lambda a, i: a[i])


def dsa_prompt(q, k, v, iq, iw, ik):
    bsz, t_len = q.shape[:2]
    topk = min(TOPK_MAX, t_len // 4)
    qb = math.gcd(Q_BLOCK, t_len)
    nb = t_len // qb

    def to_blocks(a):
        return a.reshape((bsz, nb, qb) + a.shape[2:]).swapaxes(0, 1)

    def one_block(args):
        j, q_b, iq_b, iw_b = args
        q_pos = j * qb + jnp.arange(qb)
        idx, valid = select_keys(index_scores(iq_b, iw_b, ik), q_pos, topk)
        return sparse_attend(q_b, _gather_rows(k, idx), _gather_rows(v, idx), valid)

    out = lax.map(one_block, (jnp.arange(nb), to_blocks(q), to_blocks(iq), to_blocks(iw)))
    return out.swapaxes(0, 1).reshape(bsz, t_len, ATT_W)


def dsa_sample(q, k, v, iq, iw, ik, cache_k, cache_v, cache_ik, page_table):
    bsz, t_len = q.shape[:2]
    page = cache_k.shape[1]
    past = page_table.shape[1] * page
    ik_past = cache_ik[page_table].reshape(bsz, past, IDX_DIM)
    ik_all = jnp.concatenate([ik_past, ik.astype(ik_past.dtype)], axis=1)
    q_pos = past + jnp.arange(t_len)
    topk = min(TOPK_MAX, (past + t_len) // 4)
    idx, valid = select_keys(index_scores(iq, iw, ik_all), q_pos, topk)
    in_past = (idx < past)[..., None, None]
    pidx = jnp.minimum(idx, past - 1)
    phys = page_table[jnp.arange(bsz)[:, None, None], pidx // page]
    off = pidx % page
    nidx = jnp.clip(idx - past, 0, t_len - 1)
    k_sel = jnp.where(in_past, cache_k[phys, off], _gather_rows(k, nidx).astype(cache_k.dtype))
    v_sel = jnp.where(in_past, cache_v[phys, off], _gather_rows(v, nidx).astype(cache_v.dtype))
    return sparse_attend(q, k_sel, v_sel, valid).astype(q.dtype)


def mlstm(q, k, v, ig, lf, c0, n0, m0):
    bsz, t_len = q.shape[:2]
    ch = math.gcd(ML_CHUNK, t_len)
    nc = t_len // ch

    def to_chunks(a):
        a = a.astype(F32).reshape((bsz, nc, ch) + a.shape[2:])
        return a.transpose((1, 0, 3, 2) + tuple(range(4, a.ndim)))

    causal = jnp.tril(jnp.ones((ch, ch), dtype=bool))

    def step(carry, xs):
        c, n, m = carry
        qc, kc, vc, ic, fc = xs
        b = jnp.cumsum(fc, axis=-1)
        dmat = jnp.where(causal, b[..., :, None] - b[..., None, :] + ic[..., None, :], -jnp.inf)
        m_inter = b + m[..., None]
        m_t = jnp.maximum(m_inter, jnp.max(dmat, axis=-1))
        w_intra = jnp.exp(dmat - m_t[..., None])
        w_inter = jnp.exp(m_inter - m_t)
        qk = jnp.einsum('bhtd,bhsd->bhts', qc, kc) * w_intra
        num = w_inter[..., None] * jnp.einsum('bhvd,bhtd->bhtv', c, qc) + jnp.einsum('bhts,bhsv->bhtv', qk, vc)
        den = w_inter * jnp.einsum('bhd,bhtd->bht', n, qc) + jnp.sum(qk, axis=-1)
        h = num / jnp.maximum(jnp.abs(den), jnp.exp(-m_t))[..., None]
        m_new = m_t[..., -1]
        decay = jnp.exp(b[..., -1] + m - m_new)
        w_s = jnp.exp(b[..., -1:] - b + ic - m_new[..., None])
        c_new = decay[..., None, None] * c + jnp.einsum('bhs,bhsv,bhsd->bhvd', w_s, vc, kc)
        n_new = decay[..., None] * n + jnp.einsum('bhs,bhsd->bhd', w_s, kc)
        return (c_new, n_new, m_new), h

    init = (c0.astype(F32), n0.astype(F32), m0.astype(F32))
    (c, n, m), hs = lax.scan(step, init, (to_chunks(q), to_chunks(k), to_chunks(v), to_chunks(ig), to_chunks(lf)))
    h = hs.transpose(1, 0, 3, 2, 4).reshape(bsz, t_len, ML_HEADS, ML_V)
    return h, c, n, m


def swiglu_expert(xb, wg, bg, wu, bu, wd, bd):
    g = jnp.minimum(xb @ wg + bg, SWIGLU_LIMIT)
    u = jnp.clip(xb @ wu + bu, -SWIGLU_LIMIT, SWIGLU_LIMIT)
    return ((u + 1.0) * (g * jax.nn.sigmoid(SWIGLU_ALPHA * g))) @ wd + bd


def moe(x2, w_router, b_router, w_gate, b_gate, w_up, b_up, w_down, b_down):
    n_tok = x2.shape[0]
    logits = (x2 @ w_router + b_router).astype(F32)
    top_val, top_idx = lax.top_k(logits, TOP_K)
    gates = jax.nn.softmax(top_val, axis=-1)
    n_asg = n_tok * TOP_K
    e_flat = top_idx.reshape(n_asg)
    order = jnp.argsort(e_flat)
    e_sorted = e_flat[order]
    tok_sorted = (order // TOP_K).astype(jnp.int32)
    gate_sorted = gates.reshape(n_asg)[order]
    counts = jnp.bincount(e_flat, length=N_EXPERTS)
    starts = jnp.cumsum(counts) - counts
    padded = (counts + MOE_BLOCK - 1) // MOE_BLOCK * MOE_BLOCK
    pad_end = jnp.cumsum(padded)
    pad_start = pad_end - padded
    dest = pad_start[e_sorted] + jnp.arange(n_asg) - starts[e_sorted]
    n_blk = -(-n_asg // MOE_BLOCK) + N_EXPERTS
    n_slot = n_blk * MOE_BLOCK
    slot_tok = jnp.zeros((n_slot,), jnp.int32).at[dest].set(tok_sorted)
    slot_gate = jnp.zeros((n_slot,), F32).at[dest].set(gate_sorted)
    blk_expert = jnp.minimum(jnp.searchsorted(pad_end, jnp.arange(n_blk) * MOE_BLOCK, side='right'),
                             N_EXPERTS - 1).astype(jnp.int32)

    def run_block(args):
        tok, gate, e = args
        y = swiglu_expert(x2[tok], w_gate[e], b_gate[e], w_up[e], b_up[e], w_down[e], b_down[e])
        return (y.astype(F32) * gate[:, None]).astype(x2.dtype)

    ys = lax.map(run_block, (slot_tok.reshape(n_blk, MOE_BLOCK), slot_gate.reshape(n_blk, MOE_BLOCK), blk_expert))
    return jnp.zeros_like(x2).at[slot_tok].add(ys.reshape(n_slot, D_MODEL))


def block(x, pos, attend, c0, n0, m0, g_mix, w_in, b_igate, b_fgate, g_mlstm, w_out,
          g_ffn, w_router, b_router, w_gate, b_gate, w_up, b_up, w_down, b_down):
    bsz, t_len, _ = x.shape
    h = rmsnorm(x, g_mix)
    aq, ak, av, iq, iw, ik, mq, mk, mv, mo, mi, mf = jnp.split(h @ w_in, SPLIT_POINTS, axis=-1)
    aq = rope(aq.reshape(bsz, t_len, ATT_HEADS, ATT_DIM), pos)
    ak = rope(ak.reshape(bsz, t_len, ATT_HEADS, ATT_DIM), pos)
    av = av.reshape(bsz, t_len, ATT_HEADS, ATT_DIM)
    iq = rope(iq.reshape(bsz, t_len, IDX_HEADS, IDX_DIM), pos)
    ik = rope(ik[:, :, None, :], pos)[:, :, 0, :]
    att = attend(aq, ak, av, iq, iw, ik)
    ig = softcap(mi.astype(F32) + b_igate.astype(F32))
    lf = jax.nn.log_sigmoid(softcap(mf.astype(F32) + b_fgate.astype(F32)))
    hm, c, n, m = mlstm(mq.reshape(bsz, t_len, ML_HEADS, ML_QK),
                        mk.reshape(bsz, t_len, ML_HEADS, ML_QK) * (ML_QK ** -0.5),
                        mv.reshape(bsz, t_len, ML_HEADS, ML_V), ig, lf, c0, n0, m0)
    hm = hm * lax.rsqrt(jnp.mean(hm * hm, axis=-1, keepdims=True) + EPS)
    hm = hm.reshape(bsz, t_len, ML_W) * g_mlstm.astype(F32) * jax.nn.sigmoid(mo.astype(F32))
    mix = jnp.concatenate([att.astype(x.dtype), hm.astype(x.dtype)], axis=-1)
    x = x + mix @ w_out
    ff = moe(rmsnorm(x, g_ffn).reshape(bsz * t_len, D_MODEL), w_router, b_router,
             w_gate, b_gate, w_up, b_up, w_down, b_down)
    x = x + ff.reshape(bsz, t_len, D_MODEL)
    return x, (ak, av, ik), (c, n, m)


def setup_inputs(seed: int = 0) -> dict:
    key = jax.random.key(seed)
    ks = jax.random.split(key, 32)
    n_pages = PAST_LEN // PAGE_SIZE
    n_used = DEC_BATCH * n_pages
    n_pool = n_used + max(1, n_used // 4)

    def nrm(k, shape, s):
        return jax.random.normal(k, shape, F32) * s

    page_table = jax.random.permutation(ks[0], n_pool)[:n_used].reshape(DEC_BATCH, n_pages).astype(jnp.int32)
    return {
        'x_prompt': nrm(ks[1], (BATCH, SEQ, D_MODEL), 1.0),
        'x_sample': nrm(ks[2], (DEC_BATCH, DEC_SEQ, D_MODEL), 1.0),
        'cache_k': nrm(ks[3], (DEPTH, n_pool, PAGE_SIZE, ATT_HEADS, ATT_DIM), 1.0),
        'cache_v': nrm(ks[4], (DEPTH, n_pool, PAGE_SIZE, ATT_HEADS, ATT_DIM), 1.0),
        'cache_idx_k': nrm(ks[5], (DEPTH, n_pool, PAGE_SIZE, IDX_DIM), 1.0),
        'state_C': nrm(ks[6], (DEPTH, DEC_BATCH, ML_HEADS, ML_V, ML_QK), 0.5),
        'state_n': nrm(ks[7], (DEPTH, DEC_BATCH, ML_HEADS, ML_QK), 0.5),
        'state_m': nrm(ks[8], (DEPTH, DEC_BATCH, ML_HEADS), 1.0),
        'page_table': page_table,
        'g_mix': 1.0 + nrm(ks[9], (DEPTH, D_MODEL), 0.02),
        'w_in': nrm(ks[10], (DEPTH, D_MODEL, IN_W), D_MODEL ** -0.5),
        'b_igate': nrm(ks[11], (DEPTH, ML_HEADS), 0.1),
        'b_fgate': 3.0 + nrm(ks[12], (DEPTH, ML_HEADS), 0.5),
        'g_mlstm': 1.0 + nrm(ks[13], (DEPTH, ML_W), 0.02),
        'w_out': nrm(ks[14], (DEPTH, MIX_W, D_MODEL), MIX_W ** -0.5),
        'g_ffn': 1.0 + nrm(ks[15], (DEPTH, D_MODEL), 0.02),
        'w_router': nrm(ks[16], (DEPTH, D_MODEL, N_EXPERTS), D_MODEL ** -0.5),
        'b_router': nrm(ks[17], (DEPTH, N_EXPERTS), 0.01),
        'w_gate': nrm(ks[18], (DEPTH, N_EXPERTS, D_MODEL, D_FF), D_MODEL ** -0.5),
        'b_gate': nrm(ks[19], (DEPTH, N_EXPERTS, D_FF), 0.02),
        'w_up': nrm(ks[20], (DEPTH, N_EXPERTS, D_MODEL, D_FF), D_MODEL ** -0.5),
        'b_up': nrm(ks[21], (DEPTH, N_EXPERTS, D_FF), 0.02),
        'w_down': nrm(ks[22], (DEPTH, N_EXPERTS, D_FF, D_MODEL), D_FF ** -0.5),
        'b_down': nrm(ks[23], (DEPTH, N_EXPERTS, D_MODEL), 0.02),
        'g_final': 1.0 + nrm(ks[24], (D_MODEL,), 0.02),
    }


def reference(x_prompt, x_sample, cache_k, cache_v, cache_idx_k, state_C, state_n, state_m, page_table,
              g_mix, w_in, b_igate, b_fgate, g_mlstm, w_out, g_ffn, w_router, b_router,
              w_gate, b_gate, w_up, b_up, w_down, b_down, g_final):
    page = cache_k.shape[2]
    past_len = page_table.shape[1] * page
    bp, tp, _ = x_prompt.shape
    pos_p = jnp.arange(tp)
    pos_s = past_len + jnp.arange(x_sample.shape[1])
    xp, xs = x_prompt, x_sample
    kp, vp, ikp, cp, np_, mp = [], [], [], [], [], []
    ksm, vsm, iks, csm, nsm, msm = [], [], [], [], [], []
    for l in range(DEPTH):
        w = (g_mix[l], w_in[l], b_igate[l], b_fgate[l], g_mlstm[l], w_out[l], g_ffn[l], w_router[l],
             b_router[l], w_gate[l], b_gate[l], w_up[l], b_up[l], w_down[l], b_down[l])
        c0 = jnp.zeros((bp, ML_HEADS, ML_V, ML_QK), F32)
        n0 = jnp.zeros((bp, ML_HEADS, ML_QK), F32)
        m0 = jnp.zeros((bp, ML_HEADS), F32)
        xp, (k_p, v_p, ik_p), (c_p, n_p, m_p) = block(xp, pos_p, dsa_prompt, c0, n0, m0, *w)
        attend_s = functools.partial(dsa_sample, cache_k=cache_k[l], cache_v=cache_v[l],
                                     cache_ik=cache_idx_k[l], page_table=page_table)
        xs, (k_s, v_s, ik_s), (c_s, n_s, m_s) = block(xs, pos_s, attend_s, state_C[l], state_n[l], state_m[l], *w)
        kp.append(k_p.reshape(bp, tp // page, page, ATT_HEADS, ATT_DIM))
        vp.append(v_p.reshape(bp, tp // page, page, ATT_HEADS, ATT_DIM))
        ikp.append(ik_p.reshape(bp, tp // page, page, IDX_DIM))
        cp.append(c_p); np_.append(n_p); mp.append(m_p)
        ksm.append(k_s); vsm.append(v_s); iks.append(ik_s)
        csm.append(c_s); nsm.append(n_s); msm.append(m_s)
    y_prompt = rmsnorm(xp, g_final)
    y_sample = rmsnorm(xs, g_final)
    return (y_prompt, y_sample,
            jnp.stack(kp), jnp.stack(vp), jnp.stack(ikp), jnp.stack(cp), jnp.stack(np_), jnp.stack(mp),
            jnp.stack(ksm), jnp.stack(vsm), jnp.stack(iks), jnp.stack(csm), jnp.stack(nsm), jnp.stack(msm))
```

```python
import functools
import math

import jax
import jax.numpy as jnp
import numpy as np
from jax import lax
from jax.experimental import pallas as pl
from jax.experimental.pallas import tpu as pltpu

F32 = jnp.float32
BF16 = jnp.bfloat16
I32 = jnp.int32

ATT_HEADS = 8
ATT_DIM = 64
IDX_HEADS = 8
IDX_DIM = 64
TOPK_MAX = 256
ML_HEADS = 4
ML_QK = 64
ML_V = 128
GATE_CAP = 15.0
N_EXPERTS = 32
TOP_K = 4
SWIGLU_LIMIT = 7.0
SWIGLU_ALPHA = 1.702
ROPE_THETA = 10000.0
EPS = 1e-5

ATT_W = ATT_HEADS * ATT_DIM
IDX_W = IDX_HEADS * IDX_DIM
ML_QW = ML_HEADS * ML_QK
ML_W = ML_HEADS * ML_V

LANES = 128
VMEM_LIMIT = 56 * 1024 * 1024

KEY_MASKED = -(2 ** 31)
KEY_NEG_INF = int(np.int32(np.uint32(0xFF800000)) ^ np.int32(0x7FFFFFFF))

C_AQ, C_AK, C_AV, C_IQ, C_MQ, C_MK, C_MV, C_MO, C_MISC = 0, 512, 1024, 1536, 2048, 2304, 2560, 3072, 3584
C_AQR, C_AKR, C_IQR, C_MISCR, C_END = 3712, 4224, 4736, 5248, 5376
M_IK, M_IW, M_MI, M_MF = 0, 64, 72, 76


def _cparams(sem, vmem=VMEM_LIMIT):
    return pltpu.CompilerParams(dimension_semantics=sem, vmem_limit_bytes=vmem)


def _rot_cols(w):
    d, n = w.shape
    w4 = w.reshape(d, n // 64, 2, 32)
    return jnp.stack([w4[:, :, 1, :], w4[:, :, 0, :]], axis=2).reshape(d, n)


def _build_w_in(w_in):
    d = w_in.shape[0]
    sp = np.cumsum([0, ATT_W, ATT_W, ATT_W, IDX_W, IDX_HEADS, IDX_DIM, ML_QW, ML_QW, ML_W, ML_W, ML_HEADS, ML_HEADS])
    aq, ak, av, iq, iw, ik, mq, mk, mv, mo, mi, mf = [w_in[:, sp[i]:sp[i + 1]] for i in range(12)]
    misc = jnp.concatenate([ik, iw, mi, mf, jnp.zeros((d, LANES - 80), w_in.dtype)], axis=1)
    miscr = jnp.concatenate([_rot_cols(ik), jnp.zeros((d, LANES - 64), w_in.dtype)], axis=1)
    w = jnp.concatenate([aq, ak, av, iq, mq, mk, mv, mo, misc, _rot_cols(aq), _rot_cols(ak), _rot_cols(iq), miscr], axis=1)
    assert w.shape[1] == C_END
    return w.astype(BF16)


def _rope_tables(pos):
    half = ATT_DIM // 2
    inv = ROPE_THETA ** (-jnp.arange(half, dtype=F32) / half)
    ang = pos.astype(F32)[:, None] * inv[None, :]
    cos, sin = jnp.cos(ang), jnp.sin(ang)
    cos_t = jnp.concatenate([cos, cos, cos, cos], axis=1)
    sin_t = jnp.concatenate([-sin, sin, -sin, sin], axis=1)
    return cos_t, sin_t


def _inproj_kernel(x_ref, g_ref, w_ref, cos_ref, sin_ref,
                   q_ref, kf_ref, kb_ref, vf_ref, vb_ref, iq_ref, misc_ref, mq_ref, mk_ref, mv_ref, mo_ref):
    x = x_ref[...]
    r = lax.rsqrt(jnp.mean(x * x, axis=-1, keepdims=True) + EPS)
    h = (x * r * g_ref[...]).astype(BF16)
    cos = cos_ref[...]
    sin = sin_ref[...]

    def dot(c0, n):
        return jnp.dot(h, w_ref[:, c0:c0 + n], preferred_element_type=F32)

    def rope(c0, cr, n, lane_cos, lane_sin):
        z = dot(c0, n)
        zr = dot(cr, n)
        return [z[:, j:j + LANES] * lane_cos + zr[:, j:j + LANES] * lane_sin for j in range(0, n, LANES)]

    for j, t in enumerate(rope(C_AQ, C_AQR, ATT_W, cos, sin)):
        q_ref[:, j * LANES:(j + 1) * LANES] = (t * (ATT_DIM ** -0.5)).astype(BF16)
    for j, t in enumerate(rope(C_AK, C_AKR, ATT_W, cos, sin)):
        kf_ref[:, j * LANES:(j + 1) * LANES] = t
        kb_ref[:, j * LANES:(j + 1) * LANES] = t.astype(BF16)
    v = dot(C_AV, ATT_W)
    vf_ref[...] = v
    vb_ref[...] = v.astype(BF16)
    for j, t in enumerate(rope(C_IQ, C_IQR, IDX_W, cos, sin)):
        iq_ref[:, j * LANES:(j + 1) * LANES] = t.astype(BF16)
    lane = lax.broadcasted_iota(I32, cos.shape, 1)
    mcos = jnp.where(lane < IDX_DIM, cos, 1.0)
    msin = jnp.where(lane < IDX_DIM, sin, 0.0)
    misc_ref[...] = rope(C_MISC, C_MISCR, LANES, mcos, msin)[0]
    mq_ref[...] = dot(C_MQ, ML_QW).astype(BF16)
    mk_ref[...] = (dot(C_MK, ML_QW) * (ML_QK ** -0.5)).astype(BF16)
    mv_ref[...] = dot(C_MV, ML_W).astype(BF16)
    mo_ref[...] = dot(C_MO, ML_W)


def _inproj(x2, g, w, cos_t, sin_t, tm):
    n, d = x2.shape
    tt = cos_t.shape[0]
    assert n % tm == 0 and tt % tm == 0
    nt = tt // tm
    row = lambda i: (i, 0)
    outs = [(ATT_W, BF16), (ATT_W, F32), (ATT_W, BF16), (ATT_W, F32), (ATT_W, BF16), (IDX_W, BF16), (LANES, F32),
            (ML_QW, BF16), (ML_QW, BF16), (ML_W, BF16), (ML_W, F32)]
    return pl.pallas_call(
        _inproj_kernel,
        grid=(n // tm,),
        in_specs=[pl.BlockSpec((tm, d), row),
                  pl.BlockSpec((1, d), lambda i: (0, 0)),
                  pl.BlockSpec((d, C_END), lambda i: (0, 0), pipeline_mode=pl.Buffered(1)),
                  pl.BlockSpec((tm, LANES), lambda i: (i % nt, 0)),
                  pl.BlockSpec((tm, LANES), lambda i: (i % nt, 0))],
        out_specs=[pl.BlockSpec((tm, c), row) for c, _ in outs],
        out_shape=[jax.ShapeDtypeStruct((n, c), dt) for c, dt in outs],
        compiler_params=_cparams(("parallel",)),
        name="inproj",
    )(x2, g.reshape(1, d), w, cos_t, sin_t)


def _sort_key(score):
    bits = lax.bitcast_convert_type(score, I32)
    return bits ^ ((bits >> 31) & 0x7FFFFFFF)


def _kth_largest_key(count_ge, total, k, shape):
    def bit_step(i, carry):
        t, cnt = carry
        cand = t + (jnp.int32(1) << (31 - i))
        c = count_ge(cand)
        ok = c >= k
        return jnp.where(ok, cand, t), jnp.where(ok, c, cnt)

    t0 = jnp.full(shape, KEY_MASKED, I32)
    return lax.fori_loop(0, 32, bit_step, (t0, jnp.full(shape, total, I32)))


def _dsa_prompt_kernel(qT_ref, iqT_ref, iwT_ref, kc_ref, vT_ref, ikc_ref, oT_ref,
                       keys_ref, qpad_ref, acc_ref, m_ref, l_ref, *, topk):
    j = pl.program_id(1)
    nck = j + 1
    tq = keys_ref.shape[1]
    kc_sz = kc_ref.shape[1]
    assert kc_sz == tq

    def chunk_rows(c):
        return pl.ds(pl.multiple_of(c * kc_sz, kc_sz), kc_sz)

    row = lax.broadcasted_iota(I32, (kc_sz, tq), 0)
    col = lax.broadcasted_iota(I32, (kc_sz, tq), 1)

    def score_chunk(c, _):
        ik_c = ikc_ref[c]
        acc = jnp.zeros((kc_sz, tq), F32)
        for h in range(IDX_HEADS):
            s = jnp.dot(ik_c, iqT_ref[h * IDX_DIM:(h + 1) * IDX_DIM, :], preferred_element_type=F32)
            acc = acc + iwT_ref[h:h + 1, :] * jnp.maximum(s, 0.0)
        admissible = (c < j) | (row <= col)
        keys_ref[chunk_rows(c), :] = jnp.where(admissible, _sort_key(acc), KEY_MASKED)
        return 0

    lax.fori_loop(0, nck, score_chunk, 0)

    def count_ge(cand):
        def body(c, acc):
            ge = (keys_ref[chunk_rows(c), :] >= cand).astype(I32)
            return acc + ge.reshape(kc_sz // 8, 8, tq).sum(axis=0)
        acc = lax.fori_loop(0, nck, body, jnp.zeros((8, tq), I32))
        return acc.sum(axis=0, keepdims=True)

    t_k, cnt = _kth_largest_key(count_ge, nck * kc_sz, topk, (1, tq))
    ties = (cnt > topk) & (t_k > KEY_NEG_INF)

    @pl.when(jnp.max(ties.astype(I32)) > 0)
    def _drop_late_ties():
        need = (topk - count_ge(t_k + 1)).astype(F32)
        ltri = (col < row).astype(BF16)

        def body(c, carry):
            kc = keys_ref[chunk_rows(c), :]
            eq = (kc == t_k) & ties
            eqf = jnp.where(eq, 1.0, 0.0).astype(BF16)
            rank = jnp.dot(ltri, eqf, preferred_element_type=F32) + carry
            keys_ref[chunk_rows(c), :] = jnp.where(eq & (rank >= need), KEY_MASKED, kc)
            return carry + jnp.sum(eqf.astype(F32), axis=0, keepdims=True)

        lax.fori_loop(0, nck, body, jnp.zeros((1, tq), F32))

    t_sel = jnp.maximum(t_k, KEY_NEG_INF + 1)

    hrow = lax.broadcasted_iota(I32, (LANES, tq), 0)
    for h in range(ATT_HEADS):
        pair = qT_ref[(h // 2) * LANES:(h // 2 + 1) * LANES, :].astype(F32)
        mine = (hrow >= (h % 2) * ATT_DIM) & (hrow < (h % 2 + 1) * ATT_DIM)
        qpad_ref[h] = jnp.where(mine, pair, 0.0).astype(BF16)
    acc_ref[...] = jnp.zeros_like(acc_ref)
    m_ref[...] = jnp.full(m_ref.shape, -1e30, F32)
    l_ref[...] = jnp.zeros_like(l_ref)

    def attend_chunk(c, _):
        bias = jnp.where(keys_ref[chunk_rows(c), :] >= t_sel, 0.0, -jnp.inf).astype(F32)
        for h in range(ATT_HEADS):
            kp = kc_ref[c, :, (h // 2) * LANES:(h // 2 + 1) * LANES]
            s = jnp.dot(kp, qpad_ref[h], preferred_element_type=F32) + bias
            m_old = m_ref[h:h + 1, :]
            m_new = jnp.maximum(m_old, jnp.max(s, axis=0, keepdims=True))
            alpha = jnp.exp(m_old - m_new)
            p = jnp.exp(s - m_new)
            l_ref[h:h + 1, :] = alpha * l_ref[h:h + 1, :] + jnp.sum(p, axis=0, keepdims=True)
            m_ref[h:h + 1, :] = m_new
            pv = jnp.dot(vT_ref[c, h * ATT_DIM:(h + 1) * ATT_DIM, :], p.astype(BF16), preferred_element_type=F32)
            hs = slice(h * ATT_DIM, (h + 1) * ATT_DIM)
            acc_ref[hs, :] = alpha * acc_ref[hs, :] + pv
        return 0

    lax.fori_loop(0, nck, attend_chunk, 0)
    for h in range(ATT_HEADS):
        hs = slice(h * ATT_DIM, (h + 1) * ATT_DIM)
        oT_ref[hs, :] = (acc_ref[hs, :] / l_ref[h:h + 1, :]).astype(oT_ref.dtype)


def _dsa_prompt(q, k, v, iq, ik, iw, bsz, t_len, tq):
    assert t_len % tq == 0
    nq = t_len // tq
    topk = min(TOPK_MAX, t_len // 4)

    def to_t(a):
        return a.reshape(bsz, nq, tq, a.shape[-1]).swapaxes(2, 3)

    qT, iqT, iwT, vT = to_t(q), to_t(iq), to_t(iw), to_t(v)
    kc = k.reshape(bsz, nq, tq, ATT_W)
    ikc = ik.reshape(bsz, nq, tq, IDX_DIM)
    per_q = lambda c: pl.BlockSpec((None, None, c, tq), lambda b, j: (b, j, 0, 0))
    per_b = lambda s: pl.BlockSpec((None,) + s, lambda b, j: (b, 0, 0, 0), pipeline_mode=pl.Buffered(1))
    oT = pl.pallas_call(
        functools.partial(_dsa_prompt_kernel, topk=topk),
        grid=(bsz, nq),
        in_specs=[per_q(ATT_W), per_q(IDX_W), per_q(IDX_HEADS),
                  per_b((nq, tq, ATT_W)), per_b((nq, ATT_W, tq)), per_b((nq, tq, IDX_DIM))],
        out_specs=per_q(ATT_W),
        out_shape=jax.ShapeDtypeStruct((bsz, nq, ATT_W, tq), BF16),
        scratch_shapes=[pltpu.VMEM((t_len, tq), I32), pltpu.VMEM((ATT_HEADS, LANES, tq), BF16),
                        pltpu.VMEM((ATT_W, tq), F32), pltpu.VMEM((ATT_HEADS, tq), F32), pltpu.VMEM((ATT_HEADS, tq), F32)],
        compiler_params=_cparams(("parallel", "arbitrary")),
        name="dsa_prompt",
    )(qT, iqT, iwT, kc, vT, ikc)
    return oT.swapaxes(2, 3).reshape(bsz * t_len, ATT_W)


def _softcap(a):
    return GATE_CAP * jnp.tanh(a / GATE_CAP)


def _log_sigmoid(a):
    return -(jnp.maximum(-a, 0.0) + jnp.log1p(jnp.exp(-jnp.abs(a))))


def _dot_f32(a, b):
    return jnp.dot(a, b, precision=lax.Precision.HIGHEST, preferred_element_type=F32)


def _mlstm_kernel(q_ref, kT_ref, v_ref, mo_ref, gcol_ref, grow_ref, bcol_ref, brow_ref, g_ref,
                  hm_ref, s_out_ref, m_out_ref, s_ref, m_ref):
    c = pl.program_id(1)
    L = q_ref.shape[0]

    @pl.when(c == 0)
    def _init():
        s_ref[...] = jnp.zeros_like(s_ref)
        m_ref[...] = jnp.zeros_like(m_ref)

    row = lax.broadcasted_iota(I32, (L, L), 0)
    col = lax.broadcasted_iota(I32, (L, L), 1)
    causal = col <= row
    tri_l = causal.astype(F32)
    tri_u = (row <= col).astype(F32)

    gc = _softcap(gcol_ref[...] + brow_ref[...])
    b_col = _dot_f32(tri_l, _log_sigmoid(gc))
    gr = _softcap(grow_ref[...] + bcol_ref[...])
    b_row = _dot_f32(_log_sigmoid(gr), tri_u)
    ones_col = (lax.broadcasted_iota(I32, (L, ML_V), 1) == 0).astype(BF16)

    for h in range(ML_HEADS):
        bc = b_col[:, M_MF + h:M_MF + h + 1]
        br = b_row[ML_HEADS + h:ML_HEADS + h + 1, :]
        ir = gr[h:h + 1, :]
        m_prev = m_ref[h:h + 1, 0:1]
        dmat = jnp.where(causal, bc - br + ir, -jnp.inf)
        m_inter = bc + m_prev
        m_t = jnp.maximum(m_inter, jnp.max(dmat, axis=1, keepdims=True))
        w_intra = jnp.exp(dmat - m_t)
        w_inter = jnp.exp(m_inter - m_t)
        qh = q_ref[:, h * ML_QK:(h + 1) * ML_QK]
        kth = kT_ref[h * ML_QK:(h + 1) * ML_QK, :]
        vext = jnp.concatenate([v_ref[:, h * ML_V:(h + 1) * ML_V], ones_col], axis=1)
        s_h = s_ref[h]
        qk = jnp.dot(qh, kth, preferred_element_type=F32) * w_intra
        hext = (w_inter * jnp.dot(qh, s_h.astype(BF16), preferred_element_type=F32)
                + jnp.dot(qk.astype(BF16), vext, preferred_element_type=F32))
        num = hext[:, :ML_V]
        den = hext[:, ML_V:ML_V + 1]
        hh = num / jnp.maximum(jnp.abs(den), jnp.exp(-m_t))
        hh = hh * lax.rsqrt(jnp.mean(hh * hh, axis=-1, keepdims=True) + EPS)
        hs = slice(h * ML_V, (h + 1) * ML_V)
        hm_ref[:, hs] = (hh * g_ref[:, hs] * jax.nn.sigmoid(mo_ref[:, hs])).astype(hm_ref.dtype)
        b_last = bc[L - 1:L, :]
        m_new = m_t[L - 1:L, :]
        decay = jnp.exp(b_last + m_prev - m_new)
        w_s = jnp.exp(b_last - br + ir - m_new)
        kw = (kth.astype(F32) * w_s).astype(BF16)
        s_ref[h] = decay * s_h + jnp.dot(kw, vext, preferred_element_type=F32)
        m_ref[h:h + 1, :] = jnp.broadcast_to(m_new, (1, LANES))

    @pl.when(c == pl.num_programs(1) - 1)
    def _emit_state():
        s_out_ref[...] = s_ref[...]
        m_out_ref[...] = m_ref[...]


def _gate_bias(b_igate, b_fgate):
    bias8 = jnp.concatenate([b_igate, b_fgate]).astype(F32)
    brow = jnp.zeros((1, LANES), F32).at[0, M_MI:M_MI + 2 * ML_HEADS].set(bias8)
    return bias8.reshape(2 * ML_HEADS, 1), brow


def _mlstm_prompt(mq, mk, mv, mo, misc, b_igate, b_fgate, g_mlstm, bsz, t_len, L):
    assert t_len % L == 0
    nc = t_len // L
    kT = mk.reshape(bsz, nc, L, ML_QW).swapaxes(2, 3)
    grow = misc[:, M_MI:M_MI + 2 * ML_HEADS].reshape(bsz, nc, L, 2 * ML_HEADS).swapaxes(2, 3)
    bcol, brow = _gate_bias(b_igate, b_fgate)
    rows = lambda w: pl.BlockSpec((L, w), lambda b, c: (b * nc + c, 0))
    cst = lambda s: pl.BlockSpec(s, lambda b, c: (0,) * len(s))
    hm, s_out, m_out = pl.pallas_call(
        _mlstm_kernel,
        grid=(bsz, nc),
        in_specs=[rows(ML_QW), pl.BlockSpec((None, None, ML_QW, L), lambda b, c: (b, c, 0, 0)), rows(ML_W), rows(ML_W),
                  rows(LANES), pl.BlockSpec((None, None, 2 * ML_HEADS, L), lambda b, c: (b, c, 0, 0)),
                  cst((2 * ML_HEADS, 1)), cst((1, LANES)), cst((1, ML_W))],
        out_specs=[rows(ML_W), pl.BlockSpec((None, ML_HEADS, ML_QK, 2 * ML_V), lambda b, c: (b, 0, 0, 0)),
                   pl.BlockSpec((None, 2 * ML_HEADS, LANES), lambda b, c: (b, 0, 0))],
        out_shape=[jax.ShapeDtypeStruct((bsz * t_len, ML_W), BF16),
                   jax.ShapeDtypeStruct((bsz, ML_HEADS, ML_QK, 2 * ML_V), F32),
                   jax.ShapeDtypeStruct((bsz, 2 * ML_HEADS, LANES), F32)],
        scratch_shapes=[pltpu.VMEM((ML_HEADS, ML_QK, 2 * ML_V), F32), pltpu.VMEM((2 * ML_HEADS, LANES), F32)],
        compiler_params=_cparams(("parallel", "arbitrary")),
        name="mlstm_prompt",
    )(mq, kT, mv, mo, misc, grow, bcol, brow, g_mlstm.reshape(1, ML_W))
    c_state = s_out[:, :, :, :ML_V].swapaxes(2, 3)
    n_state = s_out[:, :, :, ML_V]
    m_state = m_out[:, :ML_HEADS, 0]
    return hm, c_state, n_state, m_state


def _outproj_kernel(x_ref, att_ref, hm_ref, wo_ref, g_ref, wr_ref, br_ref, x1_ref, h2_ref, ridx_ref, rgate_ref):
    x1 = (x_ref[...]
          + jnp.dot(att_ref[...], wo_ref[:ATT_W, :], preferred_element_type=F32)
          + jnp.dot(hm_ref[...], wo_ref[ATT_W:, :], preferred_element_type=F32))
    x1_ref[...] = x1
    r = lax.rsqrt(jnp.mean(x1 * x1, axis=-1, keepdims=True) + EPS)
    h2 = x1 * r * g_ref[...]
    h2_ref[...] = h2
    logits = _dot_f32(h2, wr_ref[...]) + br_ref[...]
    lane = lax.broadcasted_iota(I32, logits.shape, 1)
    ridx = jnp.zeros(logits.shape, I32)
    vals = []
    for kk in range(TOP_K):
        mx = jnp.max(logits, axis=1, keepdims=True)
        am = jnp.min(jnp.where(logits == mx, lane, LANES), axis=1, keepdims=True)
        ridx = jnp.where(lane == kk, am, ridx)
        vals.append(mx)
        logits = jnp.where(lane == am, -jnp.inf, logits)
    es = [jnp.exp(v - vals[0]) for v in vals]
    tot = es[0] + es[1] + es[2] + es[3]
    rgate = jnp.zeros(logits.shape, F32)
    for kk in range(TOP_K):
        rgate = jnp.where(lane == kk, es[kk] / tot, rgate)
    ridx_ref[...] = ridx
    rgate_ref[...] = rgate


def _outproj(x2, att, hm, wo, g_ffn, wr, br, tm):
    n, d = x2.shape
    assert n % tm == 0
    row = lambda w: pl.BlockSpec((tm, w), lambda i: (i, 0))
    cst = lambda s: pl.BlockSpec(s, lambda i: (0, 0))
    return pl.pallas_call(
        _outproj_kernel,
        grid=(n // tm,),
        in_specs=[row(d), row(ATT_W), row(ML_W), cst((ATT_W + ML_W, d)), cst((1, d)), cst((d, LANES)), cst((1, LANES))],
        out_specs=[row(d), row(d), row(LANES), row(LANES)],
        out_shape=[jax.ShapeDtypeStruct((n, d), F32), jax.ShapeDtypeStruct((n, d), F32),
                   jax.ShapeDtypeStruct((n, LANES), I32), jax.ShapeDtypeStruct((n, LANES), F32)],
        compiler_params=_cparams(("parallel",)),
        name="outproj_router",
    )(x2, att, hm, wo, g_ffn.reshape(1, d), wr, br)


def _router_params(w_router, b_router):
    d = w_router.shape[0]
    wr = jnp.zeros((d, LANES), F32).at[:, :N_EXPERTS].set(w_router.astype(F32))
    br = jnp.full((1, LANES), -jnp.inf, F32).at[0, :N_EXPERTS].set(b_router.astype(F32))
    return wr, br


def _route(top_idx, block):
    n_tok = top_idx.shape[0]
    n_asg = n_tok * TOP_K
    e_flat = top_idx.reshape(n_asg).astype(I32)
    order = jnp.argsort(e_flat)
    e_sorted = e_flat[order]
    counts = jnp.bincount(e_flat, length=N_EXPERTS)
    starts = jnp.cumsum(counts) - counts
    padded = (counts + block - 1) // block * block
    pad_end = jnp.cumsum(padded)
    pad_start = pad_end - padded
    dest = (pad_start[e_sorted] + jnp.arange(n_asg) - starts[e_sorted]).astype(I32)
    n_blk = -(-n_asg // block) + N_EXPERTS
    n_slot = n_blk * block
    slot_tok = jnp.zeros((n_slot,), I32).at[dest].set((order // TOP_K).astype(I32))
    is_pad = jnp.ones((n_slot,), I32).at[dest].set(0)
    dump = n_asg + jnp.cumsum(is_pad) - 1
    slot_dst = dump.astype(I32).at[dest].set(order.astype(I32))
    blk_expert = jnp.minimum(jnp.searchsorted(pad_end, jnp.arange(n_blk) * block, side='right'),
                             N_EXPERTS - 1).astype(I32)
    return slot_tok.reshape(n_blk, 1, block), slot_dst.reshape(n_blk, 1, block), blk_expert


def _moe_kernel(be_ref, tok_ref, tokn_ref, dst_ref, x_hbm, wg_ref, bg_ref, wu_ref, bu_ref, wd_ref, bd_ref, ys_hbm,
                xbuf, ybuf, gsem, ssem):
    i = pl.program_id(0)
    nb = pl.num_programs(0)
    bs = xbuf.shape[1]
    slot = i % 2

    def gather(tref, s):
        def body(r, _):
            pltpu.make_async_copy(x_hbm.at[pl.ds(tref[0, r], 1)], xbuf.at[s, pl.ds(r, 1)], gsem.at[s]).start()
            return 0
        lax.fori_loop(0, bs, body, 0, unroll=8)

    def wait_gather(s):
        pltpu.make_async_copy(x_hbm.at[pl.ds(0, bs)], xbuf.at[s], gsem.at[s]).wait()

    def wait_scatter(s):
        pltpu.make_async_copy(ybuf.at[s], ys_hbm.at[pl.ds(0, bs)], ssem.at[s]).wait()

    @pl.when(i == 0)
    def _first():
        gather(tok_ref, 0)

    @pl.when(i + 1 < nb)
    def _prefetch():
        gather(tokn_ref, 1 - slot)

    wait_gather(slot)

    @pl.when(i >= 2)
    def _reuse():
        wait_scatter(slot)

    xb = xbuf[slot].astype(BF16)
    g = jnp.minimum(jnp.dot(xb, wg_ref[...], preferred_element_type=F32) + bg_ref[...], SWIGLU_LIMIT)
    u = jnp.clip(jnp.dot(xb, wu_ref[...], preferred_element_type=F32) + bu_ref[...], -SWIGLU_LIMIT, SWIGLU_LIMIT)
    act = (u + 1.0) * (g * jax.nn.sigmoid(SWIGLU_ALPHA * g))
    ybuf[slot] = jnp.dot(act.astype(BF16), wd_ref[...], preferred_element_type=F32) + bd_ref[...]

    def scatter(r, _):
        pltpu.make_async_copy(ybuf.at[slot, pl.ds(r, 1)], ys_hbm.at[pl.ds(dst_ref[0, r], 1)], ssem.at[slot]).start()
        return 0
    lax.fori_loop(0, bs, scatter, 0, unroll=8)

    @pl.when(i == nb - 1)
    def _drain():
        wait_scatter(slot)

        @pl.when(nb >= 2)
        def _():
            wait_scatter(1 - slot)


def _moe(h2, top_idx, wg, bg, wu, bu, wd, bd, block):
    n, d = h2.shape
    dff = wg.shape[2]
    slot_tok, slot_dst, blk_expert = _route(top_idx, block)
    n_blk = blk_expert.shape[0]
    n_rows = n_blk * block
    smem_blk = lambda f: pl.BlockSpec((None, 1, block), f, memory_space=pltpu.SMEM)
    wspec = lambda a, b_: pl.BlockSpec((None, a, b_), lambda i, be: (be[i], 0, 0))
    grid_spec = pltpu.PrefetchScalarGridSpec(
        num_scalar_prefetch=1,
        grid=(n_blk,),
        in_specs=[smem_blk(lambda i, be: (i, 0, 0)),
                  smem_blk(lambda i, be: (jnp.minimum(i + 1, n_blk - 1), 0, 0)),
                  smem_blk(lambda i, be: (i, 0, 0)),
                  pl.BlockSpec(memory_space=pl.ANY),
                  wspec(d, dff), wspec(1, dff), wspec(d, dff), wspec(1, dff), wspec(dff, d), wspec(1, d)],
        out_specs=pl.BlockSpec(memory_space=pl.ANY),
        scratch_shapes=[pltpu.VMEM((2, block, d), F32), pltpu.VMEM((2, block, d), F32),
                        pltpu.SemaphoreType.DMA((2,)), pltpu.SemaphoreType.DMA((2,))],
    )
    ys = pl.pallas_call(
        _moe_kernel,
        grid_spec=grid_spec,
        out_shape=jax.ShapeDtypeStruct((n_rows, d), F32),
        compiler_params=_cparams(("arbitrary",)),
        name="moe_experts",
    )(blk_expert, slot_tok, slot_tok, slot_dst, h2,
      wg, bg.reshape(N_EXPERTS, 1, dff), wu, bu.reshape(N_EXPERTS, 1, dff), wd, bd.reshape(N_EXPERTS, 1, d))
    return ys.reshape(n_rows // TOP_K, TOP_K * d)


def _combine_kernel(x1_ref, ys_ref, gate_ref, g_ref, y_ref):
    d = x1_ref.shape[1]
    x = x1_ref[...]
    for kk in range(TOP_K):
        x = x + gate_ref[:, kk:kk + 1] * ys_ref[:, kk * d:(kk + 1) * d]
    r = lax.rsqrt(jnp.mean(x * x, axis=-1, keepdims=True) + EPS)
    y_ref[...] = x * r * g_ref[...]


def _combine(x1, ys, rgate, g_final, tm):
    n, d = x1.shape
    row = lambda w: pl.BlockSpec((tm, w), lambda i: (i, 0))
    return pl.pallas_call(
        _combine_kernel,
        grid=(n // tm,),
        in_specs=[row(d), row(TOP_K * d), row(LANES), pl.BlockSpec((1, d), lambda i: (0, 0))],
        out_specs=row(d),
        out_shape=jax.ShapeDtypeStruct((n, d), F32),
        compiler_params=_cparams(("parallel",)),
        name="combine_norm",
    )(x1, ys, rgate, g_final.reshape(1, d))


def _page_copies(pt_ref, b, cache_hbm, buf, sem, first_page, n_pages):
    return [pltpu.make_async_copy(cache_hbm.at[pt_ref[b, first_page + p]], buf.at[p], sem) for p in range(n_pages)]


def _dsa_sample_select_kernel(pt_ref, iq_ref, iw_ref, ikown_ref, cache_hbm, bias_ref, kbuf, sem, *, topk):
    b = pl.program_id(0)
    nb = pl.num_programs(0)
    n_pages, page = kbuf.shape[1], kbuf.shape[2]
    past = n_pages * page
    nk = bias_ref.shape[1]
    slot = b % 2

    @pl.when(b == 0)
    def _first():
        for cp in _page_copies(pt_ref, 0, cache_hbm, kbuf.at[0], sem.at[0], 0, n_pages):
            cp.start()

    @pl.when(b + 1 < nb)
    def _prefetch():
        for cp in _page_copies(pt_ref, b + 1, cache_hbm, kbuf.at[1 - slot], sem.at[1 - slot], 0, n_pages):
            cp.start()

    for cp in _page_copies(pt_ref, b, cache_hbm, kbuf.at[slot], sem.at[slot], 0, n_pages):
        cp.wait()

    iq = iq_ref[...]
    w = iw_ref[...]
    keys = kbuf[slot].reshape(past, IDX_DIM).astype(BF16)
    s = lax.dot_general(iq, keys, (((1,), (1,)), ((), ())), preferred_element_type=F32)
    score = jnp.sum(w * jnp.maximum(s, 0.0), axis=0, keepdims=True)
    own = ikown_ref[...].astype(BF16).astype(F32)
    s_own = jnp.sum(iq.astype(F32) * own, axis=1, keepdims=True)
    score_own = jnp.sum(w * jnp.maximum(s_own, 0.0), axis=0, keepdims=True)
    tail_lane = lax.broadcasted_iota(I32, (1, nk - past), 1)
    key = jnp.concatenate([_sort_key(score),
                           jnp.where(tail_lane == 0, _sort_key(score_own), KEY_MASKED)], axis=1)
    idx = lax.broadcasted_iota(I32, (1, nk), 1)

    def count_ge(cand):
        return jnp.sum((key >= cand).astype(I32), axis=1, keepdims=True)

    t_k, _ = _kth_largest_key(count_ge, nk, topk, (1, 1))
    need = topk - count_ge(t_k + 1)
    eq = key == t_k
    nbits = int(nk).bit_length()

    def idx_step(i, jb):
        cand = jb + (jnp.int32(1) << (nbits - 1 - i))
        c = jnp.sum((eq & (idx < cand)).astype(I32), axis=1, keepdims=True)
        return jnp.where(c <= need, cand, jb)

    j_star = lax.fori_loop(0, nbits, idx_step, jnp.zeros((1, 1), I32))
    sel = ((key > t_k) | (eq & (idx < j_star))) & (key > KEY_NEG_INF)
    bias_ref[...] = jnp.where(sel, 0.0, -jnp.inf).astype(F32)


def _dsa_sample_attend_kernel(pt_ref, q_ref, kown_ref, vown_ref, bias_ref, bias_own_ref, ck_hbm, cv_hbm, o_ref,
                              kbuf, vbuf, ksem, vsem, acc_ref, m_ref, l_ref):
    b = pl.program_id(0)
    c = pl.program_id(1)
    nb, ncg = pl.num_programs(0), pl.num_programs(1)
    pg, page = kbuf.shape[1], kbuf.shape[2]
    ck = pg * page
    step = b * ncg + c
    slot = step % 2

    def copies(bb, cc, s):
        return (_page_copies(pt_ref, bb, ck_hbm, kbuf.at[s], ksem.at[s], cc * pg, pg)
                + _page_copies(pt_ref, bb, cv_hbm, vbuf.at[s], vsem.at[s], cc * pg, pg))

    @pl.when(step == 0)
    def _first():
        for cp in copies(0, 0, 0):
            cp.start()

    @pl.when(step + 1 < nb * ncg)
    def _prefetch():
        nxt = step + 1
        for cp in copies(nxt // ncg, nxt % ncg, 1 - slot):
            cp.start()

    q = q_ref[...].astype(F32)
    hd = lax.broadcasted_iota(I32, (ATT_HEADS, ATT_W), 0)
    ln = lax.broadcasted_iota(I32, (ATT_HEADS, ATT_W), 1)
    diag = (ln >= hd * ATT_DIM) & (ln < (hd + 1) * ATT_DIM)
    qblk = jnp.where(diag, q, 0.0)

    @pl.when(c == 0)
    def _init():
        acc_ref[...] = jnp.zeros_like(acc_ref)
        m_ref[...] = jnp.full(m_ref.shape, -1e30, F32)
        l_ref[...] = jnp.zeros_like(l_ref)

    for cp in copies(b, c, slot):
        cp.wait()

    kc = kbuf[slot].reshape(ck, ATT_W).astype(BF16)
    vc = vbuf[slot].reshape(ck, ATT_W).astype(BF16)
    s = lax.dot_general(qblk.astype(BF16), kc, (((1,), (1,)), ((), ())), preferred_element_type=F32)
    s = s + bias_ref[...]
    m_old = m_ref[...]
    m_new = jnp.maximum(m_old, jnp.max(s, axis=1, keepdims=True))
    alpha = jnp.exp(m_old - m_new)
    p = jnp.exp(s - m_new)
    l_ref[...] = alpha * l_ref[...] + jnp.sum(p, axis=1, keepdims=True)
    m_ref[...] = m_new
    acc_ref[...] = alpha * acc_ref[...] + jnp.dot(p.astype(BF16), vc, preferred_element_type=F32)

    @pl.when(c == ncg - 1)
    def _finish():
        k_own = kown_ref[...].astype(F32)
        v_own = vown_ref[...].astype(F32)
        s_own = jnp.sum(qblk * k_own, axis=1, keepdims=True) + bias_own_ref[:, 0:1]
        m_old = m_ref[...]
        m_new = jnp.maximum(m_old, s_own)
        alpha = jnp.exp(m_old - m_new)
        p_own = jnp.exp(s_own - m_new)
        l = alpha * l_ref[...] + p_own
        acc = alpha * acc_ref[...] + p_own * v_own
        o_ref[...] = jnp.sum(jnp.where(diag, acc / l, 0.0), axis=0, keepdims=True).astype(o_ref.dtype)


def _dsa_sample(q, kb, vb, iq, ik_own, iw, cache_k, cache_v, cache_ik, page_table, pages_per_step):
    bsz = q.shape[0]
    n_pool, page = cache_ik.shape[:2]
    n_pages = page_table.shape[1]
    past = n_pages * page
    topk = min(TOPK_MAX, (past + 1) // 4)
    nk = past + LANES
    assert n_pages % pages_per_step == 0
    blk3 = lambda s: pl.BlockSpec((None,) + s, lambda b, *_: (b, 0, 0))
    bias = pl.pallas_call(
        functools.partial(_dsa_sample_select_kernel, topk=topk),
        grid_spec=pltpu.PrefetchScalarGridSpec(
            num_scalar_prefetch=1, grid=(bsz,),
            in_specs=[blk3((IDX_HEADS, IDX_DIM)), blk3((IDX_HEADS, 1)), blk3((1, IDX_DIM)),
                      pl.BlockSpec(memory_space=pl.ANY)],
            out_specs=blk3((1, nk)),
            scratch_shapes=[pltpu.VMEM((2, n_pages, page, IDX_DIM), F32), pltpu.SemaphoreType.DMA((2,))]),
        out_shape=jax.ShapeDtypeStruct((bsz, 1, nk), F32),
        compiler_params=_cparams(("arbitrary",)),
        name="dsa_sample_select",
    )(page_table, iq.reshape(bsz, IDX_HEADS, IDX_DIM), iw.reshape(bsz, IDX_HEADS, 1), ik_own.reshape(bsz, 1, IDX_DIM), cache_ik)
    ncg = n_pages // pages_per_step
    ck = cache_k.reshape(n_pool, page, ATT_W)
    cv = cache_v.reshape(n_pool, page, ATT_W)
    keys_per_step = pages_per_step * page
    bias_past = bias[:, :, :past].reshape(bsz, ncg, 1, keys_per_step)
    bias_own = bias[:, :, past:]
    att = pl.pallas_call(
        _dsa_sample_attend_kernel,
        grid_spec=pltpu.PrefetchScalarGridSpec(
            num_scalar_prefetch=1, grid=(bsz, ncg),
            in_specs=[blk3((1, ATT_W)), blk3((1, ATT_W)), blk3((1, ATT_W)),
                      pl.BlockSpec((None, None, 1, keys_per_step), lambda b, c, *_: (b, c, 0, 0)), blk3((1, LANES)),
                      pl.BlockSpec(memory_space=pl.ANY), pl.BlockSpec(memory_space=pl.ANY)],
            out_specs=blk3((1, ATT_W)),
            scratch_shapes=[pltpu.VMEM((2, pages_per_step, page, ATT_W), F32), pltpu.VMEM((2, pages_per_step, page, ATT_W), F32),
                            pltpu.SemaphoreType.DMA((2,)), pltpu.SemaphoreType.DMA((2,)),
                            pltpu.VMEM((ATT_HEADS, ATT_W), F32), pltpu.VMEM((ATT_HEADS, 1), F32), pltpu.VMEM((ATT_HEADS, 1), F32)]),
        out_shape=jax.ShapeDtypeStruct((bsz, 1, ATT_W), BF16),
        compiler_params=_cparams(("arbitrary", "arbitrary")),
        name="dsa_sample_attend",
    )(page_table, q.reshape(bsz, 1, ATT_W), kb.reshape(bsz, 1, ATT_W), vb.reshape(bsz, 1, ATT_W), bias_past, bias_own, ck, cv)
    return att.reshape(bsz, ATT_W)


def _mlstm_step_kernel(q_ref, k_ref, v_ref, mo_ref, g_ref, gate_ref, gbias_ref, c_ref, n_ref, m_ref,
                       hm_ref, c_out_ref, n_out_ref, m_out_ref):
    gates = _softcap(gate_ref[...] + gbias_ref[...])
    lane = lax.broadcasted_iota(I32, (1, LANES), 1)
    m_out = jnp.zeros((1, LANES), F32)
    for h in range(ML_HEADS):
        ig = gates[:, M_MI + h:M_MI + h + 1]
        lf = _log_sigmoid(gates[:, M_MF + h:M_MF + h + 1])
        m_prev = m_ref[:, h:h + 1]
        q = q_ref[:, h * ML_QK:(h + 1) * ML_QK].astype(F32)
        k = k_ref[:, h * ML_QK:(h + 1) * ML_QK].astype(F32)
        v = v_ref[h].astype(F32)
        c = c_ref[h]
        n = n_ref[h:h + 1, :]
        m_inter = lf + m_prev
        m_t = jnp.maximum(m_inter, ig)
        w_intra = jnp.exp(ig - m_t)
        w_inter = jnp.exp(m_inter - m_t)
        qk = jnp.sum(q * k, axis=1, keepdims=True) * w_intra
        num = w_inter * jnp.sum(c * q, axis=1, keepdims=True) + qk * v
        den = w_inter * jnp.sum(n * q, axis=1, keepdims=True) + qk
        hh = num / jnp.maximum(jnp.abs(den), jnp.exp(-m_t))
        hh = hh * lax.rsqrt(jnp.mean(hh * hh, axis=0, keepdims=True) + EPS)
        hm_ref[h] = (hh * g_ref[h] * jax.nn.sigmoid(mo_ref[h])).astype(hm_ref.dtype)
        decay = jnp.exp(lf + m_prev - m_t)
        w_s = jnp.exp(ig - m_t)
        c_out_ref[h] = decay * c + (w_s * v) * k
        n_out_ref[h:h + 1, :] = decay * n + w_s * k
        m_out = jnp.where(lane == h, m_t, m_out)
    m_out_ref[...] = m_out


def _mlstm_step(mq, mk, mv, mo, misc, b_igate, b_fgate, g_mlstm, state_c, state_n, state_m):
    bsz = mq.shape[0]
    _, brow = _gate_bias(b_igate, b_fgate)
    b3 = lambda s: pl.BlockSpec((None,) + s, lambda b: (b,) + (0,) * len(s))
    cst = lambda s: pl.BlockSpec(s, lambda b: (0,) * len(s))
    hm, c_new, n_new, m_new = pl.pallas_call(
        _mlstm_step_kernel,
        grid=(bsz,),
        in_specs=[b3((1, ML_QW)), b3((1, ML_QW)), b3((ML_HEADS, ML_V, 1)), b3((ML_HEADS, ML_V, 1)), cst((ML_HEADS, ML_V, 1)),
                  b3((1, LANES)), cst((1, LANES)), b3((ML_HEADS, ML_V, ML_QK)), b3((ML_HEADS, ML_QK)), b3((1, ML_HEADS))],
        out_specs=[b3((ML_HEADS, ML_V, 1)), b3((ML_HEADS, ML_V, ML_QK)), b3((ML_HEADS, ML_QK)), b3((1, LANES))],
        out_shape=[jax.ShapeDtypeStruct((bsz, ML_HEADS, ML_V, 1), F32),
                   jax.ShapeDtypeStruct((bsz, ML_HEADS, ML_V, ML_QK), F32),
                   jax.ShapeDtypeStruct((bsz, ML_HEADS, ML_QK), F32),
                   jax.ShapeDtypeStruct((bsz, 1, LANES), F32)],
        compiler_params=_cparams(("parallel",)),
        name="mlstm_step",
    )(mq.reshape(bsz, 1, ML_QW), mk.reshape(bsz, 1, ML_QW), mv.astype(F32).reshape(bsz, ML_HEADS, ML_V, 1),
      mo.reshape(bsz, ML_HEADS, ML_V, 1), g_mlstm.astype(F32).reshape(ML_HEADS, ML_V, 1),
      misc.reshape(bsz, 1, LANES), brow, state_c, state_n, state_m.reshape(bsz, 1, ML_HEADS))
    return hm.reshape(bsz, ML_W).astype(BF16), c_new, n_new, m_new[:, 0, :ML_HEADS]


TQ = 256
TM_PROJ = 512
ML_CHUNK = 256
MOE_BLOCK_PROMPT = 512
MOE_BLOCK_SAMPLE = 128
SAMPLE_PAGES_PER_STEP = 16


def kernel(x_prompt, x_sample, cache_k, cache_v, cache_idx_k, state_C, state_n, state_m, page_table, g_mix, w_in, b_igate, b_fgate, g_mlstm, w_out, g_ffn, w_router, b_router, w_gate, b_gate, w_up, b_up, w_down, b_down, g_final):
    bp, tp, d = x_prompt.shape
    bs, ts, _ = x_sample.shape
    assert w_in.shape[0] == 1 and ts == 1, "one layer, one new token per sampled sequence"
    page = cache_k.shape[2]
    n_pages = page_table.shape[1]
    past = n_pages * page

    w = _build_w_in(w_in[0])
    wo = w_out[0].astype(BF16)
    wr, br = _router_params(w_router[0], b_router[0])
    moe_w = (w_gate[0].astype(BF16), b_gate[0], w_up[0].astype(BF16), b_up[0], w_down[0].astype(BF16), b_down[0])

    def tail(x2, att, hm, tm, moe_block):
        x1, h2, ridx, rgate = _outproj(x2, att, hm, wo, g_ffn[0], wr, br, tm)
        ys = _moe(h2, ridx[:, :TOP_K], *moe_w, moe_block)
        return _combine(x1, ys, rgate, g_final, tm)

    xp2 = x_prompt.reshape(bp * tp, d)
    cos_p, sin_p = _rope_tables(jnp.arange(tp))
    q, kf, kb, vf, vb, iq, misc, mq, mk, mv, mo = _inproj(xp2, g_mix[0], w, cos_p, sin_p, TM_PROJ)
    ik_p = misc[:, M_IK:M_IK + IDX_DIM]
    att = _dsa_prompt(q, kb, vb, iq, ik_p.astype(BF16), misc[:, M_IW:M_IW + IDX_HEADS], bp, tp, TQ)
    hm, c_p, n_p, m_p = _mlstm_prompt(mq, mk, mv, mo, misc, b_igate[0], b_fgate[0], g_mlstm[0], bp, tp, ML_CHUNK)
    y_prompt = tail(xp2, att, hm, TM_PROJ, MOE_BLOCK_PROMPT).reshape(bp, tp, d)

    xs2 = x_sample.reshape(bs, d)
    cos_s, sin_s = _rope_tables(jnp.full((bs,), past))
    q_s, kf_s, kb_s, vf_s, vb_s, iq_s, misc_s, mq_s, mk_s, mv_s, mo_s = _inproj(xs2, g_mix[0], w, cos_s, sin_s, bs)
    ik_s = misc_s[:, M_IK:M_IK + IDX_DIM]
    att_s = _dsa_sample(q_s, kb_s, vb_s, iq_s, ik_s, misc_s[:, M_IW:M_IW + IDX_HEADS],
                        cache_k[0], cache_v[0], cache_idx_k[0], page_table, SAMPLE_PAGES_PER_STEP)
    hm_s, c_s, n_s, m_s = _mlstm_step(mq_s, mk_s, mv_s, mo_s, misc_s, b_igate[0], b_fgate[0], g_mlstm[0],
                                      state_C[0], state_n[0], state_m[0])
    y_sample = tail(xs2, att_s, hm_s, bs, MOE_BLOCK_SAMPLE).reshape(bs, ts, d)

    return (y_prompt, y_sample,
            kf.reshape(1, bp, tp // page, page, ATT_HEADS, ATT_DIM), vf.reshape(1, bp, tp // page, page, ATT_HEADS, ATT_DIM),
            ik_p.reshape(1, bp, tp // page, page, IDX_DIM), c_p[None], n_p[None], m_p[None],
            kf_s.reshape(1, bs, ts, ATT_HEADS, ATT_DIM), vf_s.reshape(1, bs, ts, ATT_HEADS, ATT_DIM),
            ik_s.reshape(1, bs, ts, IDX_DIM), c_s[None], n_s[None], m_s[None])
```

```python
import functools
import math

import jax
import jax.numpy as jnp
import numpy as np
from jax import lax
from jax.experimental import pallas as pl
from jax.experimental.pallas import tpu as pltpu

F32 = jnp.float32
BF16 = jnp.bfloat16
I32 = jnp.int32

ATT_HEADS = 8
ATT_DIM = 64
IDX_HEADS = 8
IDX_DIM = 64
TOPK_MAX = 256
ML_HEADS = 4
ML_QK = 64
ML_V = 128
GATE_CAP = 15.0
N_EXPERTS = 32
TOP_K = 4
SWIGLU_LIMIT = 7.0
SWIGLU_ALPHA = 1.702
ROPE_THETA = 10000.0
EPS = 1e-5

ATT_W = ATT_HEADS * ATT_DIM
IDX_W = IDX_HEADS * IDX_DIM
ML_QW = ML_HEADS * ML_QK
ML_W = ML_HEADS * ML_V

LANES = 128
Q_SCALE = (ATT_DIM ** -0.5) * math.log2(math.e)
VMEM_LIMIT = 56 * 1024 * 1024

KEY_MASKED = -(2 ** 31)
KEY_NEG_INF = int(np.int32(np.uint32(0xFF800000)) ^ np.int32(0x7FFFFFFF))

C_AQ, C_AK, C_AV, C_IQ, C_MQ, C_MK, C_MV, C_MO, C_MISC = 0, 512, 1024, 1536, 2048, 2304, 2560, 3072, 3584
C_AQR, C_AKR, C_IQR, C_MISCR, C_END = 3712, 4224, 4736, 5248, 5376
M_IK, M_IW, M_MI, M_MF = 0, 64, 72, 76


def _cparams(sem, vmem=VMEM_LIMIT):
    return pltpu.CompilerParams(dimension_semantics=sem, vmem_limit_bytes=vmem)


def _rot_cols(w):
    d, n = w.shape
    w4 = w.reshape(d, n // 64, 2, 32)
    return jnp.stack([w4[:, :, 1, :], w4[:, :, 0, :]], axis=2).reshape(d, n)


def _build_w_in(w_in):
    d = w_in.shape[0]
    sp = np.cumsum([0, ATT_W, ATT_W, ATT_W, IDX_W, IDX_HEADS, IDX_DIM, ML_QW, ML_QW, ML_W, ML_W, ML_HEADS, ML_HEADS])
    aq, ak, av, iq, iw, ik, mq, mk, mv, mo, mi, mf = [w_in[:, sp[i]:sp[i + 1]] for i in range(12)]
    misc = jnp.concatenate([ik, iw, mi, mf, jnp.zeros((d, LANES - 80), w_in.dtype)], axis=1)
    miscr = jnp.concatenate([_rot_cols(ik), jnp.zeros((d, LANES - 64), w_in.dtype)], axis=1)
    w = jnp.concatenate([aq, ak, av, iq, mq, mk, mv, mo, misc, _rot_cols(aq), _rot_cols(ak), _rot_cols(iq), miscr], axis=1)
    assert w.shape[1] == C_END
    return w.astype(BF16)


def _rope_tables(pos):
    half = ATT_DIM // 2
    inv = ROPE_THETA ** (-jnp.arange(half, dtype=F32) / half)
    ang = pos.astype(F32)[:, None] * inv[None, :]
    cos, sin = jnp.cos(ang), jnp.sin(ang)
    cos_t = jnp.concatenate([cos, cos, cos, cos], axis=1)
    sin_t = jnp.concatenate([-sin, sin, -sin, sin], axis=1)
    return cos_t, sin_t


def _inproj_kernel(x_ref, g_ref, w_ref, cos_ref, sin_ref,
                   q_ref, kf_ref, kb_ref, vf_ref, vb_ref, iq_ref, misc_ref, mq_ref, mk_ref, mv_ref, mo_ref):
    x = x_ref[...]
    r = lax.rsqrt(jnp.mean(x * x, axis=-1, keepdims=True) + EPS)
    h = (x * r * g_ref[...]).astype(BF16)
    cos = cos_ref[...]
    sin = sin_ref[...]

    def dot(c0, n):
        return jnp.dot(h, w_ref[:, c0:c0 + n], preferred_element_type=F32)

    def rope(c0, cr, n, lane_cos, lane_sin):
        z = dot(c0, n)
        zr = dot(cr, n)
        return [z[:, j:j + LANES] * lane_cos + zr[:, j:j + LANES] * lane_sin for j in range(0, n, LANES)]

    for j, t in enumerate(rope(C_AQ, C_AQR, ATT_W, cos, sin)):
        q_ref[:, j * LANES:(j + 1) * LANES] = (t * Q_SCALE).astype(BF16)
    for j, t in enumerate(rope(C_AK, C_AKR, ATT_W, cos, sin)):
        kf_ref[:, j * LANES:(j + 1) * LANES] = t
        kb_ref[:, j * LANES:(j + 1) * LANES] = t.astype(BF16)
    v = dot(C_AV, ATT_W)
    vf_ref[...] = v
    vb_ref[...] = v.astype(BF16)
    for j, t in enumerate(rope(C_IQ, C_IQR, IDX_W, cos, sin)):
        iq_ref[:, j * LANES:(j + 1) * LANES] = t.astype(BF16)
    lane = lax.broadcasted_iota(I32, cos.shape, 1)
    mcos = jnp.where(lane < IDX_DIM, cos, 1.0)
    msin = jnp.where(lane < IDX_DIM, sin, 0.0)
    misc_ref[...] = rope(C_MISC, C_MISCR, LANES, mcos, msin)[0]
    mq_ref[...] = dot(C_MQ, ML_QW).astype(BF16)
    mk_ref[...] = (dot(C_MK, ML_QW) * (ML_QK ** -0.5)).astype(BF16)
    mv_ref[...] = dot(C_MV, ML_W).astype(BF16)
    mo_ref[...] = dot(C_MO, ML_W)


def _inproj(x2, g, w, cos_t, sin_t, tm):
    n, d = x2.shape
    tt = cos_t.shape[0]
    assert n % tm == 0 and tt % tm == 0
    nt = tt // tm
    row = lambda i: (i, 0)
    outs = [(ATT_W, BF16), (ATT_W, F32), (ATT_W, BF16), (ATT_W, F32), (ATT_W, BF16), (IDX_W, BF16), (LANES, F32),
            (ML_QW, BF16), (ML_QW, BF16), (ML_W, BF16), (ML_W, F32)]
    return pl.pallas_call(
        _inproj_kernel,
        grid=(n // tm,),
        in_specs=[pl.BlockSpec((tm, d), row),
                  pl.BlockSpec((1, d), lambda i: (0, 0)),
                  pl.BlockSpec((d, C_END), lambda i: (0, 0), pipeline_mode=pl.Buffered(1)),
                  pl.BlockSpec((tm, LANES), lambda i: (i % nt, 0)),
                  pl.BlockSpec((tm, LANES), lambda i: (i % nt, 0))],
        out_specs=[pl.BlockSpec((tm, c), row) for c, _ in outs],
        out_shape=[jax.ShapeDtypeStruct((n, c), dt) for c, dt in outs],
        compiler_params=_cparams(("parallel",)),
        name="inproj",
    )(x2, g.reshape(1, d), w, cos_t, sin_t)


def _sort_key(score):
    bits = lax.bitcast_convert_type(score, I32)
    return bits ^ ((bits >> 31) & 0x7FFFFFFF)


BITS_PER_CHECK = 4


def _kth_largest_key(count_ge, total, k, shape):
    def pending(cnt):
        return jnp.max((cnt != k).astype(I32))

    def bit_steps(carry):
        i, _, t, cnt = carry
        for _ in range(BITS_PER_CHECK):
            cand = t + (jnp.int32(1) << (31 - i))
            c = count_ge(cand)
            ok = c >= k
            t = jnp.where(ok, cand, t)
            cnt = jnp.where(ok, c, cnt)
            i = i + 1
        return i, pending(cnt), t, cnt

    t0 = jnp.full(shape, KEY_MASKED, I32)
    cnt0 = jnp.full(shape, total, I32)
    _, _, t, cnt = lax.while_loop(lambda c: (c[0] < 32) & (c[1] > 0), bit_steps, (jnp.int32(0), pending(cnt0), t0, cnt0))
    return t, cnt


def _dsa_prompt_kernel(qT_ref, iqT_ref, iwT_ref, kc_ref, vT_ref, ikc_ref, oT_ref,
                       keys_ref, qpad_ref, bias_ref, *head_refs, topk):
    j = pl.program_id(1)
    nck = j + 1
    tq = keys_ref.shape[1]
    kc_sz = kc_ref.shape[1]
    assert kc_sz == tq

    def chunk_rows(c):
        return pl.ds(pl.multiple_of(c * kc_sz, kc_sz), kc_sz)

    row = lax.broadcasted_iota(I32, (kc_sz, tq), 0)
    col = lax.broadcasted_iota(I32, (kc_sz, tq), 1)

    def score_chunk(c, _):
        ik_c = ikc_ref[c]
        acc = jnp.zeros((kc_sz, tq), F32)
        for h in range(IDX_HEADS):
            s = jnp.dot(ik_c, iqT_ref[h * IDX_DIM:(h + 1) * IDX_DIM, :], preferred_element_type=F32)
            acc = acc + iwT_ref[h:h + 1, :] * jnp.maximum(s, 0.0)
        admissible = (c < j) | (row <= col)
        keys_ref[chunk_rows(c), :] = jnp.where(admissible, _sort_key(acc), KEY_MASKED)
        return 0

    lax.fori_loop(0, nck, score_chunk, 0)

    def count_ge(cand):
        def body(c, acc):
            ge = (keys_ref[chunk_rows(c), :] >= cand).astype(I32)
            return acc + ge.reshape(kc_sz // 8, 8, tq).sum(axis=0)
        acc = lax.fori_loop(0, nck, body, jnp.zeros((8, tq), I32))
        return acc.sum(axis=0, keepdims=True)

    t_k, cnt = _kth_largest_key(count_ge, nck * kc_sz, topk, (1, tq))
    ties = (cnt > topk) & (t_k > KEY_NEG_INF)

    @pl.when(jnp.max(ties.astype(I32)) > 0)
    def _drop_late_ties():
        need = (topk - count_ge(t_k + 1)).astype(F32)
        ltri = (col < row).astype(BF16)

        def body(c, carry):
            kc = keys_ref[chunk_rows(c), :]
            eq = (kc == t_k) & ties
            eqf = jnp.where(eq, 1.0, 0.0).astype(BF16)
            rank = jnp.dot(ltri, eqf, preferred_element_type=F32) + carry
            keys_ref[chunk_rows(c), :] = jnp.where(eq & (rank >= need), KEY_MASKED, kc)
            return carry + jnp.sum(eqf.astype(F32), axis=0, keepdims=True)

        lax.fori_loop(0, nck, body, jnp.zeros((1, tq), F32))

    t_sel = jnp.maximum(t_k, KEY_NEG_INF + 1)

    hrow = lax.broadcasted_iota(I32, (LANES, tq), 0)
    for h in range(ATT_HEADS):
        pair = qT_ref[(h // 2) * LANES:(h // 2 + 1) * LANES, :].astype(F32)
        mine = (hrow >= (h % 2) * ATT_DIM) & (hrow < (h % 2 + 1) * ATT_DIM)
        qpad_ref[h] = jnp.where(mine, pair, 0.0).astype(BF16)
    accs, ms, ls = head_refs[:ATT_HEADS], head_refs[ATT_HEADS:2 * ATT_HEADS], head_refs[2 * ATT_HEADS:]
    for h in range(ATT_HEADS):
        accs[h][...] = jnp.zeros_like(accs[h])
        ms[h][...] = jnp.full(ms[h].shape, -1e30, F32)
        ls[h][...] = jnp.zeros_like(ls[h])

    def attend_chunk(c, _):
        bias_ref[...] = jnp.where(keys_ref[chunk_rows(c), :] >= t_sel, 0.0, -jnp.inf).astype(F32)

        def qk(h):
            kp = kc_ref[c, :, (h // 2) * LANES:(h // 2 + 1) * LANES]
            return jnp.dot(kp, qpad_ref[h], preferred_element_type=F32)

        s_next = qk(0)
        for h in range(ATT_HEADS):
            s = s_next + bias_ref[...]
            if h + 1 < ATT_HEADS:
                s_next = qk(h + 1)
            m_old = ms[h][...]
            m_new = jnp.maximum(m_old, jnp.max(s, axis=0, keepdims=True))
            alpha = jnp.exp2(m_old - m_new)
            p = jnp.exp2(s - m_new)
            ls[h][...] = alpha * ls[h][...] + jnp.sum(p, axis=0, keepdims=True)
            ms[h][...] = m_new
            pv = jnp.dot(vT_ref[c, h * ATT_DIM:(h + 1) * ATT_DIM, :], p.astype(BF16), preferred_element_type=F32)
            accs[h][...] = alpha * accs[h][...] + pv
        return 0

    lax.fori_loop(0, nck, attend_chunk, 0)
    for h in range(ATT_HEADS):
        oT_ref[h * ATT_DIM:(h + 1) * ATT_DIM, :] = (accs[h][...] / ls[h][...]).astype(oT_ref.dtype)


def _dsa_prompt(q, k, v, iq, ik, iw, bsz, t_len, tq):
    assert t_len % tq == 0
    nq = t_len // tq
    topk = min(TOPK_MAX, t_len // 4)

    def to_t(a):
        return a.reshape(bsz, nq, tq, a.shape[-1]).swapaxes(2, 3)

    qT, iqT, iwT, vT = to_t(q), to_t(iq), to_t(iw), to_t(v)
    kc = k.reshape(bsz, nq, tq, ATT_W)
    ikc = ik.reshape(bsz, nq, tq, IDX_DIM)
    per_q = lambda c: pl.BlockSpec((None, None, c, tq), lambda b, j: (b, j, 0, 0))
    per_b = lambda s: pl.BlockSpec((None,) + s, lambda b, j: (b, 0, 0, 0), pipeline_mode=pl.Buffered(1))
    oT = pl.pallas_call(
        functools.partial(_dsa_prompt_kernel, topk=topk),
        grid=(bsz, nq),
        in_specs=[per_q(ATT_W), per_q(IDX_W), per_q(IDX_HEADS),
                  per_b((nq, tq, ATT_W)), per_b((nq, ATT_W, tq)), per_b((nq, tq, IDX_DIM))],
        out_specs=per_q(ATT_W),
        out_shape=jax.ShapeDtypeStruct((bsz, nq, ATT_W, tq), BF16),
        scratch_shapes=([pltpu.VMEM((t_len, tq), I32), pltpu.VMEM((ATT_HEADS, LANES, tq), BF16), pltpu.VMEM((tq, tq), F32)]
                        + [pltpu.VMEM((ATT_DIM, tq), F32)] * ATT_HEADS + [pltpu.VMEM((1, tq), F32)] * (2 * ATT_HEADS)),
        compiler_params=_cparams(("parallel", "arbitrary")),
        name="dsa_prompt",
    )(qT, iqT, iwT, kc, vT, ikc)
    return oT.swapaxes(2, 3).reshape(bsz * t_len, ATT_W)


def _softcap(a):
    return GATE_CAP * jnp.tanh(a / GATE_CAP)


def _log_sigmoid(a):
    return -(jnp.maximum(-a, 0.0) + jnp.log1p(jnp.exp(-jnp.abs(a))))


def _dot_f32(a, b):
    return jnp.dot(a, b, precision=lax.Precision.HIGHEST, preferred_element_type=F32)


def _mlstm_kernel(q_ref, kT_ref, v_ref, mo_ref, gcol_ref, grow_ref, bcol_ref, brow_ref, g_ref,
                  hm_ref, s_out_ref, m_out_ref, s_ref, m_ref):
    c = pl.program_id(1)
    L = q_ref.shape[0]

    @pl.when(c == 0)
    def _init():
        s_ref[...] = jnp.zeros_like(s_ref)
        m_ref[...] = jnp.zeros_like(m_ref)

    row = lax.broadcasted_iota(I32, (L, L), 0)
    col = lax.broadcasted_iota(I32, (L, L), 1)
    causal = col <= row
    tri_l = causal.astype(F32)
    tri_u = (row <= col).astype(F32)

    gc = _softcap(gcol_ref[...] + brow_ref[...])
    b_col = _dot_f32(tri_l, _log_sigmoid(gc))
    gr = _softcap(grow_ref[...] + bcol_ref[...])
    b_row = _dot_f32(_log_sigmoid(gr), tri_u)
    ones_col = (lax.broadcasted_iota(I32, (L, ML_V), 1) == 0).astype(BF16)

    for h in range(ML_HEADS):
        bc = b_col[:, M_MF + h:M_MF + h + 1]
        br = b_row[ML_HEADS + h:ML_HEADS + h + 1, :]
        ir = gr[h:h + 1, :]
        m_prev = m_ref[h:h + 1, 0:1]
        dmat = jnp.where(causal, bc - br + ir, -jnp.inf)
        m_inter = bc + m_prev
        m_t = jnp.maximum(m_inter, jnp.max(dmat, axis=1, keepdims=True))
        w_intra = jnp.exp(dmat - m_t)
        w_inter = jnp.exp(m_inter - m_t)
        qh = q_ref[:, h * ML_QK:(h + 1) * ML_QK]
        kth = kT_ref[h * ML_QK:(h + 1) * ML_QK, :]
        vext = jnp.concatenate([v_ref[:, h * ML_V:(h + 1) * ML_V], ones_col], axis=1)
        s_h = s_ref[h]
        qk = jnp.dot(qh, kth, preferred_element_type=F32) * w_intra
        hext = (w_inter * jnp.dot(qh, s_h.astype(BF16), preferred_element_type=F32)
                + jnp.dot(qk.astype(BF16), vext, preferred_element_type=F32))
        num = hext[:, :ML_V]
        den = hext[:, ML_V:ML_V + 1]
        hh = num / jnp.maximum(jnp.abs(den), jnp.exp(-m_t))
        hh = hh * lax.rsqrt(jnp.mean(hh * hh, axis=-1, keepdims=True) + EPS)
        hs = slice(h * ML_V, (h + 1) * ML_V)
        hm_ref[:, hs] = (hh * g_ref[:, hs] * jax.nn.sigmoid(mo_ref[:, hs])).astype(hm_ref.dtype)
        b_last = bc[L - 1:L, :]
        m_new = m_t[L - 1:L, :]
        decay = jnp.exp(b_last + m_prev - m_new)
        w_s = jnp.exp(b_last - br + ir - m_new)
        kw = (kth.astype(F32) * w_s).astype(BF16)
        s_ref[h] = decay * s_h + jnp.dot(kw, vext, preferred_element_type=F32)
        m_ref[h:h + 1, :] = jnp.broadcast_to(m_new, (1, LANES))

    @pl.when(c == pl.num_programs(1) - 1)
    def _emit_state():
        s_out_ref[...] = s_ref[...]
        m_out_ref[...] = m_ref[...]


def _gate_bias(b_igate, b_fgate):
    bias8 = jnp.concatenate([b_igate, b_fgate]).astype(F32)
    brow = jnp.zeros((1, LANES), F32).at[0, M_MI:M_MI + 2 * ML_HEADS].set(bias8)
    return bias8.reshape(2 * ML_HEADS, 1), brow


def _mlstm_prompt(mq, mk, mv, mo, misc, b_igate, b_fgate, g_mlstm, bsz, t_len, L):
    assert t_len % L == 0
    nc = t_len // L
    kT = mk.reshape(bsz, nc, L, ML_QW).swapaxes(2, 3)
    grow = misc[:, M_MI:M_MI + 2 * ML_HEADS].reshape(bsz, nc, L, 2 * ML_HEADS).swapaxes(2, 3)
    bcol, brow = _gate_bias(b_igate, b_fgate)
    rows = lambda w: pl.BlockSpec((L, w), lambda b, c: (b * nc + c, 0))
    cst = lambda s: pl.BlockSpec(s, lambda b, c: (0,) * len(s))
    hm, s_out, m_out = pl.pallas_call(
        _mlstm_kernel,
        grid=(bsz, nc),
        in_specs=[rows(ML_QW), pl.BlockSpec((None, None, ML_QW, L), lambda b, c: (b, c, 0, 0)), rows(ML_W), rows(ML_W),
                  rows(LANES), pl.BlockSpec((None, None, 2 * ML_HEADS, L), lambda b, c: (b, c, 0, 0)),
                  cst((2 * ML_HEADS, 1)), cst((1, LANES)), cst((1, ML_W))],
        out_specs=[rows(ML_W), pl.BlockSpec((None, ML_HEADS, ML_QK, 2 * ML_V), lambda b, c: (b, 0, 0, 0)),
                   pl.BlockSpec((None, 2 * ML_HEADS, LANES), lambda b, c: (b, 0, 0))],
        out_shape=[jax.ShapeDtypeStruct((bsz * t_len, ML_W), BF16),
                   jax.ShapeDtypeStruct((bsz, ML_HEADS, ML_QK, 2 * ML_V), F32),
                   jax.ShapeDtypeStruct((bsz, 2 * ML_HEADS, LANES), F32)],
        scratch_shapes=[pltpu.VMEM((ML_HEADS, ML_QK, 2 * ML_V), F32), pltpu.VMEM((2 * ML_HEADS, LANES), F32)],
        compiler_params=_cparams(("parallel", "arbitrary")),
        name="mlstm_prompt",
    )(mq, kT, mv, mo, misc, grow, bcol, brow, g_mlstm.reshape(1, ML_W))
    c_state = s_out[:, :, :, :ML_V].swapaxes(2, 3)
    n_state = s_out[:, :, :, ML_V]
    m_state = m_out[:, :ML_HEADS, 0]
    return hm, c_state, n_state, m_state


def _outproj_kernel(x_ref, att_ref, hm_ref, wo_ref, g_ref, wr_ref, br_ref, x1_ref, h2_ref, ridx_ref, rgate_ref):
    x1 = (x_ref[...]
          + jnp.dot(att_ref[...], wo_ref[:ATT_W, :], preferred_element_type=F32)
          + jnp.dot(hm_ref[...], wo_ref[ATT_W:, :], preferred_element_type=F32))
    x1_ref[...] = x1
    r = lax.rsqrt(jnp.mean(x1 * x1, axis=-1, keepdims=True) + EPS)
    h2 = x1 * r * g_ref[...]
    h2_ref[...] = h2
    logits = _dot_f32(h2, wr_ref[...]) + br_ref[...]
    lane = lax.broadcasted_iota(I32, logits.shape, 1)
    ridx = jnp.zeros(logits.shape, I32)
    vals = []
    for kk in range(TOP_K):
        mx = jnp.max(logits, axis=1, keepdims=True)
        am = jnp.min(jnp.where(logits == mx, lane, LANES), axis=1, keepdims=True)
        ridx = jnp.where(lane == kk, am, ridx)
        vals.append(mx)
        logits = jnp.where(lane == am, -jnp.inf, logits)
    es = [jnp.exp(v - vals[0]) for v in vals]
    tot = es[0] + es[1] + es[2] + es[3]
    rgate = jnp.zeros(logits.shape, F32)
    for kk in range(TOP_K):
        rgate = jnp.where(lane == kk, es[kk] / tot, rgate)
    ridx_ref[...] = ridx
    rgate_ref[...] = rgate


def _outproj(x2, att, hm, wo, g_ffn, wr, br, tm):
    n, d = x2.shape
    assert n % tm == 0
    row = lambda w: pl.BlockSpec((tm, w), lambda i: (i, 0))
    cst = lambda s: pl.BlockSpec(s, lambda i: (0, 0))
    return pl.pallas_call(
        _outproj_kernel,
        grid=(n // tm,),
        in_specs=[row(d), row(ATT_W), row(ML_W), cst((ATT_W + ML_W, d)), cst((1, d)), cst((d, LANES)), cst((1, LANES))],
        out_specs=[row(d), row(d), row(LANES), row(LANES)],
        out_shape=[jax.ShapeDtypeStruct((n, d), F32), jax.ShapeDtypeStruct((n, d), F32),
                   jax.ShapeDtypeStruct((n, LANES), I32), jax.ShapeDtypeStruct((n, LANES), F32)],
        compiler_params=_cparams(("parallel",)),
        name="outproj_router",
    )(x2, att, hm, wo, g_ffn.reshape(1, d), wr, br)


def _router_params(w_router, b_router):
    d = w_router.shape[0]
    wr = jnp.zeros((d, LANES), F32).at[:, :N_EXPERTS].set(w_router.astype(F32))
    br = jnp.full((1, LANES), -jnp.inf, F32).at[0, :N_EXPERTS].set(b_router.astype(F32))
    return wr, br


def _route(top_idx, block):
    n_tok = top_idx.shape[0]
    n_asg = n_tok * TOP_K
    assert n_asg % block == 0
    n_blocks = n_asg // block
    e_flat = top_idx.reshape(n_asg).astype(I32)
    order = jnp.argsort(e_flat).astype(I32)
    slot_tok = order // TOP_K
    slot_dst = (order % TOP_K) * n_tok + slot_tok
    counts = jnp.bincount(e_flat, length=N_EXPERTS).astype(I32)
    ends = jnp.cumsum(counts).astype(I32)
    starts = ends - counts
    blk_lo = jnp.arange(n_blocks, dtype=I32) * block
    blk_hi = blk_lo + block
    e_first = jnp.sum(ends[None, :] <= blk_lo[:, None], axis=1).astype(I32)
    e_last = jnp.sum(starts[None, :] < blk_hi[:, None], axis=1).astype(I32) - 1
    n_per = e_last - e_first + 1
    item_start = jnp.cumsum(n_per).astype(I32) - n_per
    n_items = n_blocks + N_EXPERTS - 1
    it = jnp.arange(n_items, dtype=I32)
    blk = jnp.sum(item_start[None, :] <= it[:, None], axis=1).astype(I32) - 1
    k_in = it - item_start[blk]
    real = k_in < n_per[blk]
    exp = jnp.minimum(e_first[blk] + k_in, N_EXPERTS - 1)
    lo = jnp.where(real, jnp.clip(starts[exp], blk_lo[blk], blk_hi[blk]) - blk_lo[blk], 0)
    hi = jnp.where(real, jnp.clip(ends[exp], blk_lo[blk], blk_hi[blk]) - blk_lo[blk], 0)
    exp = jnp.where(real, exp, e_last[n_blocks - 1])
    items = (blk, exp.astype(I32), lo.astype(I32), hi.astype(I32), (k_in == 0).astype(I32))
    return slot_tok.reshape(n_blocks, 1, block), slot_dst.reshape(n_blocks, 1, block), items


def _moe_kernel(blk_ref, exp_ref, lo_ref, hi_ref, first_ref, tok_ref, tokn_ref, dst_ref, x_hbm,
                wg_ref, bg_ref, wu_ref, bu_ref, wd_ref, bd_ref, ys_hbm, xbuf, ybuf, gsem, ssem, *, n_blocks):
    i = pl.program_id(0)
    n_items = pl.num_programs(0)
    bs = xbuf.shape[1]
    b = blk_ref[i]
    lo, hi = lo_ref[i], hi_ref[i]
    first = first_ref[i] > 0
    xs = b % 2
    ysl = i % 2

    def gather(tref, s):
        def body(r, _):
            pltpu.make_async_copy(x_hbm.at[pl.ds(tref[0, r], 1)], xbuf.at[s, pl.ds(r, 1)], gsem.at[s]).start()
            return 0
        lax.fori_loop(0, bs, body, 0, unroll=8)

    def wait_scatter(s, n):
        p = bs
        while p >= 1:
            @pl.when((n & p) != 0)
            def _(p=p):
                pltpu.make_async_copy(ybuf.at[s, pl.ds(0, p)], ys_hbm.at[pl.ds(0, p)], ssem.at[s]).wait()
            p //= 2

    @pl.when(i == 0)
    def _first_block():
        gather(tok_ref, 0)

    @pl.when(first & (b + 1 < n_blocks))
    def _prefetch_next_block():
        gather(tokn_ref, 1 - xs)

    @pl.when(first)
    def _rows_ready():
        pltpu.make_async_copy(x_hbm.at[pl.ds(0, bs)], xbuf.at[xs], gsem.at[xs]).wait()

    @pl.when(i >= 2)
    def _result_buffer_free():
        wait_scatter(ysl, hi_ref[i - 2] - lo_ref[i - 2])

    @pl.when(hi > lo)
    def _compute():
        xb = xbuf[xs].astype(BF16)
        g = jnp.minimum(jnp.dot(xb, wg_ref[...], preferred_element_type=F32) + bg_ref[...], SWIGLU_LIMIT)
        u = jnp.clip(jnp.dot(xb, wu_ref[...], preferred_element_type=F32) + bu_ref[...], -SWIGLU_LIMIT, SWIGLU_LIMIT)
        act = (u + 1.0) * (g * jax.nn.sigmoid(SWIGLU_ALPHA * g))
        ybuf[ysl] = jnp.dot(act.astype(BF16), wd_ref[...], preferred_element_type=F32) + bd_ref[...]

        def scatter(r, _):
            pltpu.make_async_copy(ybuf.at[ysl, pl.ds(r, 1)], ys_hbm.at[pl.ds(dst_ref[0, r], 1)], ssem.at[ysl]).start()
            return 0
        lax.fori_loop(lo, hi, scatter, 0)

    @pl.when(i == n_items - 1)
    def _drain():
        wait_scatter(ysl, hi - lo)

        @pl.when(i >= 1)
        def _():
            wait_scatter(1 - ysl, hi_ref[i - 1] - lo_ref[i - 1])


def _moe(h2, top_idx, wg, bg, wu, bu, wd, bd, block):
    n, d = h2.shape
    dff = wg.shape[2]
    slot_tok, slot_dst, items = _route(top_idx, block)
    n_blocks = slot_tok.shape[0]
    n_items = items[0].shape[0]
    smem_blk = lambda f: pl.BlockSpec((None, 1, block), f, memory_space=pltpu.SMEM)
    wspec = lambda a, b_: pl.BlockSpec((None, a, b_), lambda i, blk, exp, *_: (exp[i], 0, 0))
    grid_spec = pltpu.PrefetchScalarGridSpec(
        num_scalar_prefetch=5,
        grid=(n_items,),
        in_specs=[smem_blk(lambda i, blk, *_: (blk[i], 0, 0)),
                  smem_blk(lambda i, blk, *_: (jnp.minimum(blk[i] + 1, n_blocks - 1), 0, 0)),
                  smem_blk(lambda i, blk, *_: (blk[i], 0, 0)),
                  pl.BlockSpec(memory_space=pl.ANY),
                  wspec(d, dff), wspec(1, dff), wspec(d, dff), wspec(1, dff), wspec(dff, d), wspec(1, d)],
        out_specs=pl.BlockSpec(memory_space=pl.ANY),
        scratch_shapes=[pltpu.VMEM((2, block, d), F32), pltpu.VMEM((2, block, d), F32),
                        pltpu.SemaphoreType.DMA((2,)), pltpu.SemaphoreType.DMA((2,))],
    )
    return pl.pallas_call(
        functools.partial(_moe_kernel, n_blocks=n_blocks),
        grid_spec=grid_spec,
        out_shape=jax.ShapeDtypeStruct((n * TOP_K, d), F32),
        compiler_params=_cparams(("arbitrary",)),
        name="moe_experts",
    )(*items, slot_tok, slot_tok, slot_dst, h2,
      wg, bg.reshape(N_EXPERTS, 1, dff), wu, bu.reshape(N_EXPERTS, 1, dff), wd, bd.reshape(N_EXPERTS, 1, d))


def _combine_kernel(x1_ref, y0_ref, y1_ref, y2_ref, y3_ref, gate_ref, g_ref, y_ref):
    x = x1_ref[...]
    for kk, ys_ref in enumerate((y0_ref, y1_ref, y2_ref, y3_ref)):
        x = x + gate_ref[:, kk:kk + 1] * ys_ref[...]
    r = lax.rsqrt(jnp.mean(x * x, axis=-1, keepdims=True) + EPS)
    y_ref[...] = x * r * g_ref[...]


def _combine(x1, ys, rgate, g_final, tm):
    n, d = x1.shape
    nt = n // tm
    row = lambda w: pl.BlockSpec((tm, w), lambda i: (i, 0))
    ys_specs = [pl.BlockSpec((tm, d), lambda i, j=j: (j * nt + i, 0)) for j in range(TOP_K)]
    return pl.pallas_call(
        _combine_kernel,
        grid=(nt,),
        in_specs=[row(d)] + ys_specs + [row(LANES), pl.BlockSpec((1, d), lambda i: (0, 0))],
        out_specs=row(d),
        out_shape=jax.ShapeDtypeStruct((n, d), F32),
        compiler_params=_cparams(("parallel",)),
        name="combine_norm",
    )(x1, ys, ys, ys, ys, rgate, g_final.reshape(1, d))


def _page_copies(pt_ref, b, cache_hbm, buf, sem, first_page, n_pages):
    return [pltpu.make_async_copy(cache_hbm.at[pt_ref[b, first_page + p]], buf.at[p], sem) for p in range(n_pages)]


def _dsa_sample_select_kernel(pt_ref, iq_ref, iw_ref, ikown_ref, cache_hbm, bias_ref, kbuf, sem, *, topk):
    b = pl.program_id(0)
    nb = pl.num_programs(0)
    n_pages, page = kbuf.shape[1], kbuf.shape[2]
    past = n_pages * page
    nk = bias_ref.shape[1]
    slot = b % 2

    @pl.when(b == 0)
    def _first():
        for cp in _page_copies(pt_ref, 0, cache_hbm, kbuf.at[0], sem.at[0], 0, n_pages):
            cp.start()

    @pl.when(b + 1 < nb)
    def _prefetch():
        for cp in _page_copies(pt_ref, b + 1, cache_hbm, kbuf.at[1 - slot], sem.at[1 - slot], 0, n_pages):
            cp.start()

    for cp in _page_copies(pt_ref, b, cache_hbm, kbuf.at[slot], sem.at[slot], 0, n_pages):
        cp.wait()

    iq = iq_ref[...]
    w = iw_ref[...]
    keys = kbuf[slot].reshape(past, IDX_DIM).astype(BF16)
    s = lax.dot_general(iq, keys, (((1,), (1,)), ((), ())), preferred_element_type=F32)
    score = jnp.sum(w * jnp.maximum(s, 0.0), axis=0, keepdims=True)
    own = ikown_ref[...].astype(BF16).astype(F32)
    s_own = jnp.sum(iq.astype(F32) * own, axis=1, keepdims=True)
    score_own = jnp.sum(w * jnp.maximum(s_own, 0.0), axis=0, keepdims=True)
    tail_lane = lax.broadcasted_iota(I32, (1, nk - past), 1)
    key = jnp.concatenate([_sort_key(score),
                           jnp.where(tail_lane == 0, _sort_key(score_own), KEY_MASKED)], axis=1)
    idx = lax.broadcasted_iota(I32, (1, nk), 1)

    def count_ge(cand):
        return jnp.sum((key >= cand).astype(I32), axis=1, keepdims=True)

    t_k, _ = _kth_largest_key(count_ge, nk, topk, (1, 1))
    need = topk - count_ge(t_k + 1)
    eq = key == t_k
    nbits = int(nk).bit_length()

    def idx_step(i, jb):
        cand = jb + (jnp.int32(1) << (nbits - 1 - i))
        c = jnp.sum((eq & (idx < cand)).astype(I32), axis=1, keepdims=True)
        return jnp.where(c <= need, cand, jb)

    j_star = lax.fori_loop(0, nbits, idx_step, jnp.zeros((1, 1), I32))
    sel = ((key > t_k) | (eq & (idx < j_star))) & (key > KEY_NEG_INF)
    bias_ref[...] = jnp.where(sel, 0.0, -jnp.inf).astype(F32)


def _dsa_sample_attend_kernel(row_ref, meta_ref, q_ref, kown_ref, vown_ref, ck_hbm, cv_hbm, o_ref,
                              kbuf, vbuf, ksem, vsem):
    b = pl.program_id(0)
    nb = pl.num_programs(0)
    nsel = kbuf.shape[1]
    slot = b % 2

    def gather(bb, s):
        def body(r, _):
            rid = row_ref[bb, r]
            pltpu.make_async_copy(ck_hbm.at[rid], kbuf.at[s, r], ksem.at[s]).start()
            pltpu.make_async_copy(cv_hbm.at[rid], vbuf.at[s, r], vsem.at[s]).start()
            return 0
        lax.fori_loop(0, nsel, body, 0, unroll=8)

    @pl.when(b == 0)
    def _first():
        gather(0, 0)

    @pl.when(b + 1 < nb)
    def _prefetch():
        gather(b + 1, 1 - slot)

    pltpu.make_async_copy(ck_hbm.at[pl.ds(0, nsel)], kbuf.at[slot], ksem.at[slot]).wait()
    pltpu.make_async_copy(cv_hbm.at[pl.ds(0, nsel)], vbuf.at[slot], vsem.at[slot]).wait()

    q8 = q_ref[...]
    s = jnp.sum(kbuf[slot] * q8, axis=-1, keepdims=True)
    s = jnp.where(lax.broadcasted_iota(I32, s.shape, 0) < meta_ref[b, 0], s, -jnp.inf)
    s_own = jnp.sum(kown_ref[...] * q8, axis=-1, keepdims=True)
    s_own = jnp.where(meta_ref[b, 1] > 0, s_own, -jnp.inf)
    m = jnp.maximum(jnp.maximum(jnp.max(s, axis=0), s_own), -1e30)
    p = jnp.exp2(s - m)
    p_own = jnp.exp2(s_own - m)
    l = jnp.sum(p, axis=0) + p_own
    o = jnp.sum(p * vbuf[slot], axis=0) + p_own * vown_ref[...]
    o_ref[...] = o / l


def _dsa_sample(q, k_own, v_own, iq, ik_own, iw, cache_k, cache_v, cache_ik, page_table):
    bsz = q.shape[0]
    n_pool, page = cache_ik.shape[:2]
    n_pages = page_table.shape[1]
    past = n_pages * page
    topk = min(TOPK_MAX, (past + 1) // 4)
    nk = past + LANES
    blk3 = lambda s: pl.BlockSpec((None,) + s, lambda b, *_: (b, 0, 0))
    bias = pl.pallas_call(
        functools.partial(_dsa_sample_select_kernel, topk=topk),
        grid_spec=pltpu.PrefetchScalarGridSpec(
            num_scalar_prefetch=1, grid=(bsz,),
            in_specs=[blk3((IDX_HEADS, IDX_DIM)), blk3((IDX_HEADS, 1)), blk3((1, IDX_DIM)),
                      pl.BlockSpec(memory_space=pl.ANY)],
            out_specs=blk3((1, nk)),
            scratch_shapes=[pltpu.VMEM((2, n_pages, page, IDX_DIM), F32), pltpu.SemaphoreType.DMA((2,))]),
        out_shape=jax.ShapeDtypeStruct((bsz, 1, nk), F32),
        compiler_params=_cparams(("arbitrary",)),
        name="dsa_sample_select",
    )(page_table, iq.reshape(bsz, IDX_HEADS, IDX_DIM), iw.reshape(bsz, IDX_HEADS, 1), ik_own.reshape(bsz, 1, IDX_DIM), cache_ik)
    sel = bias[:, 0, :] == 0.0
    idx = jnp.sort(jnp.where(sel, jnp.arange(nk, dtype=I32), nk), axis=1)[:, :topk]
    gidx = jnp.minimum(idx, past - 1)
    rows = jnp.take_along_axis(page_table, gidx // page, axis=1) * page + gidx % page
    meta = jnp.stack([jnp.sum(idx < past, axis=1), sel[:, past]], axis=1).astype(I32)
    head = lambda a: a.astype(F32).reshape(bsz, ATT_HEADS, ATT_DIM)
    att = pl.pallas_call(
        _dsa_sample_attend_kernel,
        grid_spec=pltpu.PrefetchScalarGridSpec(
            num_scalar_prefetch=2, grid=(bsz,),
            in_specs=[blk3((ATT_HEADS, ATT_DIM)), blk3((ATT_HEADS, ATT_DIM)), blk3((ATT_HEADS, ATT_DIM)),
                      pl.BlockSpec(memory_space=pl.ANY), pl.BlockSpec(memory_space=pl.ANY)],
            out_specs=blk3((ATT_HEADS, ATT_DIM)),
            scratch_shapes=[pltpu.VMEM((2, topk, ATT_HEADS, ATT_DIM), F32), pltpu.VMEM((2, topk, ATT_HEADS, ATT_DIM), F32),
                            pltpu.SemaphoreType.DMA((2,)), pltpu.SemaphoreType.DMA((2,))]),
        out_shape=jax.ShapeDtypeStruct((bsz, ATT_HEADS, ATT_DIM), F32),
        compiler_params=_cparams(("arbitrary",)),
        name="dsa_sample_attend",
    )(rows.astype(I32), meta, head(q), head(k_own), head(v_own),
      cache_k.reshape(n_pool * page, ATT_HEADS, ATT_DIM), cache_v.reshape(n_pool * page, ATT_HEADS, ATT_DIM))
    return att.reshape(bsz, ATT_W).astype(BF16)


def _mlstm_step_kernel(q_ref, k_ref, v_ref, mo_ref, g_ref, gate_ref, gbias_ref, c_ref, n_ref, m_ref,
                       hm_ref, c_out_ref, n_out_ref, m_out_ref):
    gates = _softcap(gate_ref[...] + gbias_ref[...])
    lane = lax.broadcasted_iota(I32, (1, LANES), 1)
    m_out = jnp.zeros((1, LANES), F32)
    for h in range(ML_HEADS):
        ig = gates[:, M_MI + h:M_MI + h + 1]
        lf = _log_sigmoid(gates[:, M_MF + h:M_MF + h + 1])
        m_prev = m_ref[:, h:h + 1]
        q = q_ref[:, h * ML_QK:(h + 1) * ML_QK].astype(F32)
        k = k_ref[:, h * ML_QK:(h + 1) * ML_QK].astype(F32)
        v = v_ref[h].astype(F32)
        c = c_ref[h]
        n = n_ref[h:h + 1, :]
        m_inter = lf + m_prev
        m_t = jnp.maximum(m_inter, ig)
        w_intra = jnp.exp(ig - m_t)
        w_inter = jnp.exp(m_inter - m_t)
        qk = jnp.sum(q * k, axis=1, keepdims=True) * w_intra
        num = w_inter * jnp.sum(c * q, axis=1, keepdims=True) + qk * v
        den = w_inter * jnp.sum(n * q, axis=1, keepdims=True) + qk
        hh = num / jnp.maximum(jnp.abs(den), jnp.exp(-m_t))
        hh = hh * lax.rsqrt(jnp.mean(hh * hh, axis=0, keepdims=True) + EPS)
        hm_ref[h] = (hh * g_ref[h] * jax.nn.sigmoid(mo_ref[h])).astype(hm_ref.dtype)
        decay = jnp.exp(lf + m_prev - m_t)
        w_s = jnp.exp(ig - m_t)
        c_out_ref[h] = decay * c + (w_s * v) * k
        n_out_ref[h:h + 1, :] = decay * n + w_s * k
        m_out = jnp.where(lane == h, m_t, m_out)
    m_out_ref[...] = m_out


def _mlstm_step(mq, mk, mv, mo, misc, b_igate, b_fgate, g_mlstm, state_c, state_n, state_m):
    bsz = mq.shape[0]
    _, brow = _gate_bias(b_igate, b_fgate)
    b3 = lambda s: pl.BlockSpec((None,) + s, lambda b: (b,) + (0,) * len(s))
    cst = lambda s: pl.BlockSpec(s, lambda b: (0,) * len(s))
    hm, c_new, n_new, m_new = pl.pallas_call(
        _mlstm_step_kernel,
        grid=(bsz,),
        in_specs=[b3((1, ML_QW)), b3((1, ML_QW)), b3((ML_HEADS, ML_V, 1)), b3((ML_HEADS, ML_V, 1)), cst((ML_HEADS, ML_V, 1)),
                  b3((1, LANES)), cst((1, LANES)), b3((ML_HEADS, ML_V, ML_QK)), b3((ML_HEADS, ML_QK)), b3((1, ML_HEADS))],
        out_specs=[b3((ML_HEADS, ML_V, 1)), b3((ML_HEADS, ML_V, ML_QK)), b3((ML_HEADS, ML_QK)), b3((1, LANES))],
        out_shape=[jax.ShapeDtypeStruct((bsz, ML_HEADS, ML_V, 1), F32),
                   jax.ShapeDtypeStruct((bsz, ML_HEADS, ML_V, ML_QK), F32),
                   jax.ShapeDtypeStruct((bsz, ML_HEADS, ML_QK), F32),
                   jax.ShapeDtypeStruct((bsz, 1, LANES), F32)],
        compiler_params=_cparams(("parallel",)),
        name="mlstm_step",
    )(mq.reshape(bsz, 1, ML_QW), mk.reshape(bsz, 1, ML_QW), mv.astype(F32).reshape(bsz, ML_HEADS, ML_V, 1),
      mo.reshape(bsz, ML_HEADS, ML_V, 1), g_mlstm.astype(F32).reshape(ML_HEADS, ML_V, 1),
      misc.reshape(bsz, 1, LANES), brow, state_c, state_n, state_m.reshape(bsz, 1, ML_HEADS))
    return hm.reshape(bsz, ML_W).astype(BF16), c_new, n_new, m_new[:, 0, :ML_HEADS]


TQ = 512
TM_PROJ = 512
ML_CHUNK = 256
MOE_BLOCK_PROMPT = 512
MOE_BLOCK_SAMPLE = 128


def kernel(x_prompt, x_sample, cache_k, cache_v, cache_idx_k, state_C, state_n, state_m, page_table, g_mix, w_in, b_igate, b_fgate, g_mlstm, w_out, g_ffn, w_router, b_router, w_gate, b_gate, w_up, b_up, w_down, b_down, g_final):
    bp, tp, d = x_prompt.shape
    bs, ts, _ = x_sample.shape
    assert w_in.shape[0] == 1 and ts == 1, "one layer, one new token per sampled sequence"
    page = cache_k.shape[2]
    n_pages = page_table.shape[1]
    past = n_pages * page

    w = _build_w_in(w_in[0])
    wo = w_out[0].astype(BF16)
    wr, br = _router_params(w_router[0], b_router[0])
    moe_w = (w_gate[0].astype(BF16), b_gate[0], w_up[0].astype(BF16), b_up[0], w_down[0].astype(BF16), b_down[0])

    def tail(x2, att, hm, tm, moe_block):
        x1, h2, ridx, rgate = _outproj(x2, att, hm, wo, g_ffn[0], wr, br, tm)
        ys = _moe(h2, ridx[:, :TOP_K], *moe_w, moe_block)
        return _combine(x1, ys, rgate, g_final, tm)

    xp2 = x_prompt.reshape(bp * tp, d)
    cos_p, sin_p = _rope_tables(jnp.arange(tp))
    q, kf, kb, vf, vb, iq, misc, mq, mk, mv, mo = _inproj(xp2, g_mix[0], w, cos_p, sin_p, TM_PROJ)
    ik_p = misc[:, M_IK:M_IK + IDX_DIM]
    att = _dsa_prompt(q, kb, vb, iq, ik_p.astype(BF16), misc[:, M_IW:M_IW + IDX_HEADS], bp, tp, TQ)
    hm, c_p, n_p, m_p = _mlstm_prompt(mq, mk, mv, mo, misc, b_igate[0], b_fgate[0], g_mlstm[0], bp, tp, ML_CHUNK)
    y_prompt = tail(xp2, att, hm, TM_PROJ, MOE_BLOCK_PROMPT).reshape(bp, tp, d)

    xs2 = x_sample.reshape(bs, d)
    cos_s, sin_s = _rope_tables(jnp.full((bs,), past))
    q_s, kf_s, kb_s, vf_s, vb_s, iq_s, misc_s, mq_s, mk_s, mv_s, mo_s = _inproj(xs2, g_mix[0], w, cos_s, sin_s, bs)
    ik_s = misc_s[:, M_IK:M_IK + IDX_DIM]
    att_s = _dsa_sample(q_s, kf_s, vf_s, iq_s, ik_s, misc_s[:, M_IW:M_IW + IDX_HEADS],
                        cache_k[0], cache_v[0], cache_idx_k[0], page_table)
    hm_s, c_s, n_s, m_s = _mlstm_step(mq_s, mk_s, mv_s, mo_s, misc_s, b_igate[0], b_fgate[0], g_mlstm[0],
                                      state_C[0], state_n[0], state_m[0])
    y_sample = tail(xs2, att_s, hm_s, bs, min(MOE_BLOCK_SAMPLE, bs * TOP_K)).reshape(bs, ts, d)

    return (y_prompt, y_sample,
            kf.reshape(1, bp, tp // page, page, ATT_HEADS, ATT_DIM), vf.reshape(1, bp, tp // page, page, ATT_HEADS, ATT_DIM),
            ik_p.reshape(1, bp, tp // page, page, IDX_DIM), c_p[None], n_p[None], m_p[None],
            kf_s.reshape(1, bs, ts, ATT_HEADS, ATT_DIM), vf_s.reshape(1, bs, ts, ATT_HEADS, ATT_DIM),
            ik_s.reshape(1, bs, ts, IDX_DIM), c_s[None], n_s[None], m_s[None])
```

```python
import functools
import math

import jax
import jax.numpy as jnp
import numpy as np
from jax import lax
from jax.experimental import pallas as pl
from jax.experimental.pallas import tpu as pltpu

F32 = jnp.float32
BF16 = jnp.bfloat16
I32 = jnp.int32

ATT_HEADS = 8
ATT_DIM = 64
IDX_HEADS = 8
IDX_DIM = 64
TOPK_MAX = 256
ML_HEADS = 4
ML_QK = 64
ML_V = 128
GATE_CAP = 15.0
N_EXPERTS = 32
TOP_K = 4
SWIGLU_LIMIT = 7.0
SWIGLU_ALPHA = 1.702
ROPE_THETA = 10000.0
EPS = 1e-5

ATT_W = ATT_HEADS * ATT_DIM
IDX_W = IDX_HEADS * IDX_DIM
ML_QW = ML_HEADS * ML_QK
ML_W = ML_HEADS * ML_V

LANES = 128
Q_SCALE = (ATT_DIM ** -0.5) * math.log2(math.e)
VMEM_LIMIT = 56 * 1024 * 1024

KEY_MASKED = -(2 ** 31)
KEY_NEG_INF = int(np.int32(np.uint32(0xFF800000)) ^ np.int32(0x7FFFFFFF))

C_AQ, C_AK, C_AV, C_IQ, C_MQ, C_MK, C_MV, C_MO, C_MISC = 0, 512, 1024, 1536, 2048, 2304, 2560, 3072, 3584
C_AQR, C_AKR, C_IQR, C_MISCR, C_END = 3712, 4224, 4736, 5248, 5376
M_IK, M_IW, M_MI, M_MF = 0, 64, 72, 76


def _cparams(sem, vmem=VMEM_LIMIT):
    return pltpu.CompilerParams(dimension_semantics=sem, vmem_limit_bytes=vmem)


def _rot_cols(w):
    d, n = w.shape
    w4 = w.reshape(d, n // 64, 2, 32)
    return jnp.stack([w4[:, :, 1, :], w4[:, :, 0, :]], axis=2).reshape(d, n)


def _build_w_in(w_in):
    d = w_in.shape[0]
    sp = np.cumsum([0, ATT_W, ATT_W, ATT_W, IDX_W, IDX_HEADS, IDX_DIM, ML_QW, ML_QW, ML_W, ML_W, ML_HEADS, ML_HEADS])
    aq, ak, av, iq, iw, ik, mq, mk, mv, mo, mi, mf = [w_in[:, sp[i]:sp[i + 1]] for i in range(12)]
    misc = jnp.concatenate([ik, iw, mi, mf, jnp.zeros((d, LANES - 80), w_in.dtype)], axis=1)
    miscr = jnp.concatenate([_rot_cols(ik), jnp.zeros((d, LANES - 64), w_in.dtype)], axis=1)
    w = jnp.concatenate([aq, ak, av, iq, mq, mk, mv, mo, misc, _rot_cols(aq), _rot_cols(ak), _rot_cols(iq), miscr], axis=1)
    assert w.shape[1] == C_END
    return w.astype(BF16)


def _rope_tables(pos):
    half = ATT_DIM // 2
    inv = ROPE_THETA ** (-jnp.arange(half, dtype=F32) / half)
    ang = pos.astype(F32)[:, None] * inv[None, :]
    cos, sin = jnp.cos(ang), jnp.sin(ang)
    cos_t = jnp.concatenate([cos, cos, cos, cos], axis=1)
    sin_t = jnp.concatenate([-sin, sin, -sin, sin], axis=1)
    return cos_t, sin_t


def _inproj_kernel(x_ref, g_ref, w_ref, cos_ref, sin_ref,
                   q_ref, kf_ref, kb_ref, vf_ref, vb_ref, iq_ref, misc_ref, mq_ref, mk_ref, mv_ref, mo_ref):
    x = x_ref[...]
    r = lax.rsqrt(jnp.mean(x * x, axis=-1, keepdims=True) + EPS)
    h = (x * r * g_ref[...]).astype(BF16)
    cos = cos_ref[...]
    sin = sin_ref[...]

    def dot(c0, n):
        return jnp.dot(h, w_ref[:, c0:c0 + n], preferred_element_type=F32)

    def rope(c0, cr, n, lane_cos, lane_sin):
        z = dot(c0, n)
        zr = dot(cr, n)
        return [z[:, j:j + LANES] * lane_cos + zr[:, j:j + LANES] * lane_sin for j in range(0, n, LANES)]

    for j, t in enumerate(rope(C_AQ, C_AQR, ATT_W, cos, sin)):
        q_ref[:, j * LANES:(j + 1) * LANES] = (t * Q_SCALE).astype(BF16)
    for j, t in enumerate(rope(C_AK, C_AKR, ATT_W, cos, sin)):
        kf_ref[:, j * LANES:(j + 1) * LANES] = t
        kb_ref[:, j * LANES:(j + 1) * LANES] = t.astype(BF16)
    v = dot(C_AV, ATT_W)
    vf_ref[...] = v
    vb_ref[...] = v.astype(BF16)
    for j, t in enumerate(rope(C_IQ, C_IQR, IDX_W, cos, sin)):
        iq_ref[:, j * LANES:(j + 1) * LANES] = t.astype(BF16)
    lane = lax.broadcasted_iota(I32, cos.shape, 1)
    mcos = jnp.where(lane < IDX_DIM, cos, 1.0)
    msin = jnp.where(lane < IDX_DIM, sin, 0.0)
    misc_ref[...] = rope(C_MISC, C_MISCR, LANES, mcos, msin)[0]
    mq_ref[...] = dot(C_MQ, ML_QW).astype(BF16)
    mk_ref[...] = (dot(C_MK, ML_QW) * (ML_QK ** -0.5)).astype(BF16)
    mv_ref[...] = dot(C_MV, ML_W).astype(BF16)
    mo_ref[...] = dot(C_MO, ML_W)


def _inproj(x2, g, w, cos_t, sin_t, tm):
    n, d = x2.shape
    tt = cos_t.shape[0]
    assert n % tm == 0 and tt % tm == 0
    nt = tt // tm
    row = lambda i: (i, 0)
    outs = [(ATT_W, BF16), (ATT_W, F32), (ATT_W, BF16), (ATT_W, F32), (ATT_W, BF16), (IDX_W, BF16), (LANES, F32),
            (ML_QW, BF16), (ML_QW, BF16), (ML_W, BF16), (ML_W, F32)]
    return pl.pallas_call(
        _inproj_kernel,
        grid=(n // tm,),
        in_specs=[pl.BlockSpec((tm, d), row),
                  pl.BlockSpec((1, d), lambda i: (0, 0)),
                  pl.BlockSpec((d, C_END), lambda i: (0, 0), pipeline_mode=pl.Buffered(1)),
                  pl.BlockSpec((tm, LANES), lambda i: (i % nt, 0)),
                  pl.BlockSpec((tm, LANES), lambda i: (i % nt, 0))],
        out_specs=[pl.BlockSpec((tm, c), row) for c, _ in outs],
        out_shape=[jax.ShapeDtypeStruct((n, c), dt) for c, dt in outs],
        compiler_params=_cparams(("parallel",)),
        name="inproj",
    )(x2, g.reshape(1, d), w, cos_t, sin_t)


def _sort_key(score):
    bits = lax.bitcast_convert_type(score, I32)
    return bits ^ ((bits >> 31) & 0x7FFFFFFF)


BITS_PER_CHECK = 4
BINADE = 1 << 23
MAX_DESCENT = 6


def _kth_largest_key(count_ge, total, k, shape, kmax=None):
    def pending(cnt):
        return jnp.max((cnt != k).astype(I32))

    t0 = jnp.full(shape, KEY_MASKED, I32)
    cnt0 = jnp.full(shape, total, I32)
    i0 = jnp.int32(0)
    if kmax is not None:
        def short(c):
            return jnp.max((c < k).astype(I32))

        def descend(carry):
            n, _, p, c = carry
            p = jnp.where(c < k, p - BINADE, p)
            c = count_ge(p)
            return n + 1, short(c), p, c

        p0 = kmax & jnp.int32(-BINADE)
        c0 = count_ge(p0)
        _, still_short, p, c = lax.while_loop(lambda s: (s[0] < MAX_DESCENT) & (s[1] > 0), descend,
                                              (jnp.int32(0), short(c0), p0, c0))
        found = still_short == 0
        t0 = jnp.where(found, p, t0)
        cnt0 = jnp.where(found, c, cnt0)
        i0 = jnp.where(found, 8, 0).astype(I32)

    def bit_steps(carry):
        i, _, t, cnt = carry
        for _ in range(BITS_PER_CHECK):
            cand = t + (jnp.int32(1) << (31 - i))
            c = count_ge(cand)
            ok = c >= k
            t = jnp.where(ok, cand, t)
            cnt = jnp.where(ok, c, cnt)
            i = i + 1
        return i, pending(cnt), t, cnt

    _, _, t, cnt = lax.while_loop(lambda c: (c[0] < 32) & (c[1] > 0), bit_steps, (i0, pending(cnt0), t0, cnt0))
    return t, cnt


def _dsa_prompt_kernel(qT_ref, iqT_ref, iwT_ref, kc_ref, vT_ref, ikc_ref, oT_ref,
                       keys_ref, qpad_ref, bias_ref, *head_refs, topk):
    j = pl.program_id(1)
    nck = j + 1
    tq = keys_ref.shape[1]
    kc_sz = kc_ref.shape[1]
    assert kc_sz == tq

    def chunk_rows(c):
        return pl.ds(pl.multiple_of(c * kc_sz, kc_sz), kc_sz)

    row = lax.broadcasted_iota(I32, (kc_sz, tq), 0)
    col = lax.broadcasted_iota(I32, (kc_sz, tq), 1)

    def score_chunk(c, kmax):
        ik_c = ikc_ref[c]
        acc = jnp.zeros((kc_sz, tq), F32)
        for h in range(IDX_HEADS):
            s = jnp.dot(ik_c, iqT_ref[h * IDX_DIM:(h + 1) * IDX_DIM, :], preferred_element_type=F32)
            acc = acc + iwT_ref[h:h + 1, :] * jnp.maximum(s, 0.0)
        admissible = (c < j) | (row <= col)
        key = jnp.where(admissible, _sort_key(acc), KEY_MASKED)
        keys_ref[chunk_rows(c), :] = key
        return jnp.maximum(kmax, key.reshape(kc_sz // 8, 8, tq).max(axis=0))

    kmax = lax.fori_loop(0, nck, score_chunk, jnp.full((8, tq), KEY_MASKED, I32)).max(axis=0, keepdims=True)

    def count_ge(cand):
        def body(c, acc):
            ge = (keys_ref[chunk_rows(c), :] >= cand).astype(I32)
            return acc + ge.reshape(kc_sz // 8, 8, tq).sum(axis=0)
        acc = lax.fori_loop(0, nck, body, jnp.zeros((8, tq), I32))
        return acc.sum(axis=0, keepdims=True)

    t_k, cnt = _kth_largest_key(count_ge, nck * kc_sz, topk, (1, tq), kmax)
    ties = (cnt > topk) & (t_k > KEY_NEG_INF)

    @pl.when(jnp.max(ties.astype(I32)) > 0)
    def _drop_late_ties():
        need = (topk - count_ge(t_k + 1)).astype(F32)
        ltri = (col < row).astype(BF16)

        def body(c, carry):
            kc = keys_ref[chunk_rows(c), :]
            eq = (kc == t_k) & ties
            eqf = jnp.where(eq, 1.0, 0.0).astype(BF16)
            rank = jnp.dot(ltri, eqf, preferred_element_type=F32) + carry
            keys_ref[chunk_rows(c), :] = jnp.where(eq & (rank >= need), KEY_MASKED, kc)
            return carry + jnp.sum(eqf.astype(F32), axis=0, keepdims=True)

        lax.fori_loop(0, nck, body, jnp.zeros((1, tq), F32))

    t_sel = jnp.maximum(t_k, KEY_NEG_INF + 1)

    hrow = lax.broadcasted_iota(I32, (LANES, tq), 0)
    for h in range(ATT_HEADS):
        pair = qT_ref[(h // 2) * LANES:(h // 2 + 1) * LANES, :].astype(F32)
        mine = (hrow >= (h % 2) * ATT_DIM) & (hrow < (h % 2 + 1) * ATT_DIM)
        qpad_ref[h] = jnp.where(mine, pair, 0.0).astype(BF16)
    accs, ms, ls = head_refs[:ATT_HEADS], head_refs[ATT_HEADS:2 * ATT_HEADS], head_refs[2 * ATT_HEADS:]
    for h in range(ATT_HEADS):
        accs[h][...] = jnp.zeros_like(accs[h])
        ms[h][...] = jnp.full(ms[h].shape, -1e30, F32)
        ls[h][...] = jnp.zeros_like(ls[h])

    def attend_chunk(c, _):
        bias_ref[...] = jnp.where(keys_ref[chunk_rows(c), :] >= t_sel, 0.0, -jnp.inf).astype(F32)

        def qk(h):
            kp = kc_ref[c, :, (h // 2) * LANES:(h // 2 + 1) * LANES]
            return jnp.dot(kp, qpad_ref[h], preferred_element_type=F32)

        s_next = qk(0)
        for h in range(ATT_HEADS):
            s = s_next + bias_ref[...]
            if h + 1 < ATT_HEADS:
                s_next = qk(h + 1)
            m_old = ms[h][...]
            m_new = jnp.maximum(m_old, jnp.max(s, axis=0, keepdims=True))
            alpha = jnp.exp2(m_old - m_new)
            p = jnp.exp2(s - m_new)
            ls[h][...] = alpha * ls[h][...] + jnp.sum(p, axis=0, keepdims=True)
            ms[h][...] = m_new
            pv = jnp.dot(vT_ref[c, h * ATT_DIM:(h + 1) * ATT_DIM, :], p.astype(BF16), preferred_element_type=F32)
            accs[h][...] = alpha * accs[h][...] + pv
        return 0

    lax.fori_loop(0, nck, attend_chunk, 0)
    for h in range(ATT_HEADS):
        oT_ref[h * ATT_DIM:(h + 1) * ATT_DIM, :] = (accs[h][...] / ls[h][...]).astype(oT_ref.dtype)


def _dsa_prompt(q, k, v, iq, ik, iw, bsz, t_len, tq):
    assert t_len % tq == 0
    nq = t_len // tq
    topk = min(TOPK_MAX, t_len // 4)

    def to_t(a):
        return a.reshape(bsz, nq, tq, a.shape[-1]).swapaxes(2, 3)

    qT, iqT, iwT, vT = to_t(q), to_t(iq), to_t(iw), to_t(v)
    kc = k.reshape(bsz, nq, tq, ATT_W)
    ikc = ik.reshape(bsz, nq, tq, IDX_DIM)
    per_q = lambda c: pl.BlockSpec((None, None, c, tq), lambda b, j: (b, j, 0, 0))
    per_b = lambda s: pl.BlockSpec((None,) + s, lambda b, j: (b, 0, 0, 0), pipeline_mode=pl.Buffered(1))
    oT = pl.pallas_call(
        functools.partial(_dsa_prompt_kernel, topk=topk),
        grid=(bsz, nq),
        in_specs=[per_q(ATT_W), per_q(IDX_W), per_q(IDX_HEADS),
                  per_b((nq, tq, ATT_W)), per_b((nq, ATT_W, tq)), per_b((nq, tq, IDX_DIM))],
        out_specs=per_q(ATT_W),
        out_shape=jax.ShapeDtypeStruct((bsz, nq, ATT_W, tq), BF16),
        scratch_shapes=([pltpu.VMEM((t_len, tq), I32), pltpu.VMEM((ATT_HEADS, LANES, tq), BF16), pltpu.VMEM((tq, tq), F32)]
                        + [pltpu.VMEM((ATT_DIM, tq), F32)] * ATT_HEADS + [pltpu.VMEM((1, tq), F32)] * (2 * ATT_HEADS)),
        compiler_params=_cparams(("parallel", "arbitrary")),
        name="dsa_prompt",
    )(qT, iqT, iwT, kc, vT, ikc)
    return oT.swapaxes(2, 3).reshape(bsz * t_len, ATT_W)


def _softcap(a):
    return GATE_CAP * jnp.tanh(a / GATE_CAP)


def _log_sigmoid(a):
    return -(jnp.maximum(-a, 0.0) + jnp.log1p(jnp.exp(-jnp.abs(a))))


def _dot_f32(a, b):
    return jnp.dot(a, b, precision=lax.Precision.HIGHEST, preferred_element_type=F32)


def _mlstm_kernel(q_ref, kT_ref, v_ref, mo_ref, gcol_ref, grow_ref, bcol_ref, brow_ref, g_ref,
                  hm_ref, s_out_ref, m_out_ref, s_ref, m_ref):
    c = pl.program_id(1)
    L = q_ref.shape[0]

    @pl.when(c == 0)
    def _init():
        s_ref[...] = jnp.zeros_like(s_ref)
        m_ref[...] = jnp.zeros_like(m_ref)

    row = lax.broadcasted_iota(I32, (L, L), 0)
    col = lax.broadcasted_iota(I32, (L, L), 1)
    causal = col <= row
    tri_l = causal.astype(F32)
    tri_u = (row <= col).astype(F32)

    gc = _softcap(gcol_ref[...] + brow_ref[...])
    b_col = _dot_f32(tri_l, _log_sigmoid(gc))
    gr = _softcap(grow_ref[...] + bcol_ref[...])
    b_row = _dot_f32(_log_sigmoid(gr), tri_u)
    ones_col = (lax.broadcasted_iota(I32, (L, ML_V), 1) == 0).astype(BF16)

    for h in range(ML_HEADS):
        bc = b_col[:, M_MF + h:M_MF + h + 1]
        br = b_row[ML_HEADS + h:ML_HEADS + h + 1, :]
        ir = gr[h:h + 1, :]
        m_prev = m_ref[h:h + 1, 0:1]
        dmat = jnp.where(causal, bc - br + ir, -jnp.inf)
        m_inter = bc + m_prev
        m_t = jnp.maximum(m_inter, jnp.max(dmat, axis=1, keepdims=True))
        w_intra = jnp.exp(dmat - m_t)
        w_inter = jnp.exp(m_inter - m_t)
        qh = q_ref[:, h * ML_QK:(h + 1) * ML_QK]
        kth = kT_ref[h * ML_QK:(h + 1) * ML_QK, :]
        vext = jnp.concatenate([v_ref[:, h * ML_V:(h + 1) * ML_V], ones_col], axis=1)
        s_h = s_ref[h]
        qk = jnp.dot(qh, kth, preferred_element_type=F32) * w_intra
        hext = (w_inter * jnp.dot(qh, s_h.astype(BF16), preferred_element_type=F32)
                + jnp.dot(qk.astype(BF16), vext, preferred_element_type=F32))
        num = hext[:, :ML_V]
        den = hext[:, ML_V:ML_V + 1]
        hh = num / jnp.maximum(jnp.abs(den), jnp.exp(-m_t))
        hh = hh * lax.rsqrt(jnp.mean(hh * hh, axis=-1, keepdims=True) + EPS)
        hs = slice(h * ML_V, (h + 1) * ML_V)
        hm_ref[:, hs] = (hh * g_ref[:, hs] * jax.nn.sigmoid(mo_ref[:, hs])).astype(hm_ref.dtype)
        b_last = bc[L - 1:L, :]
        m_new = m_t[L - 1:L, :]
        decay = jnp.exp(b_last + m_prev - m_new)
        w_s = jnp.exp(b_last - br + ir - m_new)
        kw = (kth.astype(F32) * w_s).astype(BF16)
        s_ref[h] = decay * s_h + jnp.dot(kw, vext, preferred_element_type=F32)
        m_ref[h:h + 1, :] = jnp.broadcast_to(m_new, (1, LANES))

    @pl.when(c == pl.num_programs(1) - 1)
    def _emit_state():
        s_out_ref[...] = s_ref[...]
        m_out_ref[...] = m_ref[...]


def _gate_bias(b_igate, b_fgate):
    bias8 = jnp.concatenate([b_igate, b_fgate]).astype(F32)
    brow = jnp.zeros((1, LANES), F32).at[0, M_MI:M_MI + 2 * ML_HEADS].set(bias8)
    return bias8.reshape(2 * ML_HEADS, 1), brow


def _mlstm_prompt(mq, mk, mv, mo, misc, b_igate, b_fgate, g_mlstm, bsz, t_len, L):
    assert t_len % L == 0
    nc = t_len // L
    kT = mk.reshape(bsz, nc, L, ML_QW).swapaxes(2, 3)
    grow = misc[:, M_MI:M_MI + 2 * ML_HEADS].reshape(bsz, nc, L, 2 * ML_HEADS).swapaxes(2, 3)
    bcol, brow = _gate_bias(b_igate, b_fgate)
    rows = lambda w: pl.BlockSpec((L, w), lambda b, c: (b * nc + c, 0))
    cst = lambda s: pl.BlockSpec(s, lambda b, c: (0,) * len(s))
    hm, s_out, m_out = pl.pallas_call(
        _mlstm_kernel,
        grid=(bsz, nc),
        in_specs=[rows(ML_QW), pl.BlockSpec((None, None, ML_QW, L), lambda b, c: (b, c, 0, 0)), rows(ML_W), rows(ML_W),
                  rows(LANES), pl.BlockSpec((None, None, 2 * ML_HEADS, L), lambda b, c: (b, c, 0, 0)),
                  cst((2 * ML_HEADS, 1)), cst((1, LANES)), cst((1, ML_W))],
        out_specs=[rows(ML_W), pl.BlockSpec((None, ML_HEADS, ML_QK, 2 * ML_V), lambda b, c: (b, 0, 0, 0)),
                   pl.BlockSpec((None, 2 * ML_HEADS, LANES), lambda b, c: (b, 0, 0))],
        out_shape=[jax.ShapeDtypeStruct((bsz * t_len, ML_W), BF16),
                   jax.ShapeDtypeStruct((bsz, ML_HEADS, ML_QK, 2 * ML_V), F32),
                   jax.ShapeDtypeStruct((bsz, 2 * ML_HEADS, LANES), F32)],
        scratch_shapes=[pltpu.VMEM((ML_HEADS, ML_QK, 2 * ML_V), F32), pltpu.VMEM((2 * ML_HEADS, LANES), F32)],
        compiler_params=_cparams(("parallel", "arbitrary")),
        name="mlstm_prompt",
    )(mq, kT, mv, mo, misc, grow, bcol, brow, g_mlstm.reshape(1, ML_W))
    c_state = s_out[:, :, :, :ML_V].swapaxes(2, 3)
    n_state = s_out[:, :, :, ML_V]
    m_state = m_out[:, :ML_HEADS, 0]
    return hm, c_state, n_state, m_state


def _outproj_kernel(x_ref, att_ref, hm_ref, wo_ref, g_ref, wr_ref, br_ref, x1_ref, h2_ref, ridx_ref, rgate_ref):
    x1 = (x_ref[...]
          + jnp.dot(att_ref[...], wo_ref[:ATT_W, :], preferred_element_type=F32)
          + jnp.dot(hm_ref[...], wo_ref[ATT_W:, :], preferred_element_type=F32))
    x1_ref[...] = x1
    r = lax.rsqrt(jnp.mean(x1 * x1, axis=-1, keepdims=True) + EPS)
    h2 = x1 * r * g_ref[...]
    h2_ref[...] = h2
    logits = _dot_f32(h2, wr_ref[...]) + br_ref[...]
    lane = lax.broadcasted_iota(I32, logits.shape, 1)
    ridx = jnp.zeros(logits.shape, I32)
    vals = []
    for kk in range(TOP_K):
        mx = jnp.max(logits, axis=1, keepdims=True)
        am = jnp.min(jnp.where(logits == mx, lane, LANES), axis=1, keepdims=True)
        ridx = jnp.where(lane == kk, am, ridx)
        vals.append(mx)
        logits = jnp.where(lane == am, -jnp.inf, logits)
    es = [jnp.exp(v - vals[0]) for v in vals]
    tot = es[0] + es[1] + es[2] + es[3]
    rgate = jnp.zeros(logits.shape, F32)
    for kk in range(TOP_K):
        rgate = jnp.where(lane == kk, es[kk] / tot, rgate)
    ridx_ref[...] = ridx
    rgate_ref[...] = rgate


def _outproj(x2, att, hm, wo, g_ffn, wr, br, tm):
    n, d = x2.shape
    assert n % tm == 0
    row = lambda w: pl.BlockSpec((tm, w), lambda i: (i, 0))
    cst = lambda s: pl.BlockSpec(s, lambda i: (0, 0))
    return pl.pallas_call(
        _outproj_kernel,
        grid=(n // tm,),
        in_specs=[row(d), row(ATT_W), row(ML_W), cst((ATT_W + ML_W, d)), cst((1, d)), cst((d, LANES)), cst((1, LANES))],
        out_specs=[row(d), row(d), row(LANES), row(LANES)],
        out_shape=[jax.ShapeDtypeStruct((n, d), F32), jax.ShapeDtypeStruct((n, d), F32),
                   jax.ShapeDtypeStruct((n, LANES), I32), jax.ShapeDtypeStruct((n, LANES), F32)],
        compiler_params=_cparams(("parallel",)),
        name="outproj_router",
    )(x2, att, hm, wo, g_ffn.reshape(1, d), wr, br)


def _router_params(w_router, b_router):
    d = w_router.shape[0]
    wr = jnp.zeros((d, LANES), F32).at[:, :N_EXPERTS].set(w_router.astype(F32))
    br = jnp.full((1, LANES), -jnp.inf, F32).at[0, :N_EXPERTS].set(b_router.astype(F32))
    return wr, br


def _route(top_idx, block):
    n_tok = top_idx.shape[0]
    n_asg = n_tok * TOP_K
    assert n_asg % block == 0
    n_blocks = n_asg // block
    e_flat = top_idx.reshape(n_asg).astype(I32)
    order = jnp.argsort(e_flat).astype(I32)
    slot_tok = order // TOP_K
    slot_dst = (order % TOP_K) * n_tok + slot_tok
    counts = jnp.bincount(e_flat, length=N_EXPERTS).astype(I32)
    ends = jnp.cumsum(counts).astype(I32)
    starts = ends - counts
    blk_lo = jnp.arange(n_blocks, dtype=I32) * block
    blk_hi = blk_lo + block
    e_first = jnp.sum(ends[None, :] <= blk_lo[:, None], axis=1).astype(I32)
    e_last = jnp.sum(starts[None, :] < blk_hi[:, None], axis=1).astype(I32) - 1
    n_per = e_last - e_first + 1
    item_start = jnp.cumsum(n_per).astype(I32) - n_per
    n_items = n_blocks + N_EXPERTS - 1
    it = jnp.arange(n_items, dtype=I32)
    blk = jnp.sum(item_start[None, :] <= it[:, None], axis=1).astype(I32) - 1
    k_in = it - item_start[blk]
    real = k_in < n_per[blk]
    exp = jnp.minimum(e_first[blk] + k_in, N_EXPERTS - 1)
    lo = jnp.where(real, jnp.clip(starts[exp], blk_lo[blk], blk_hi[blk]) - blk_lo[blk], 0)
    hi = jnp.where(real, jnp.clip(ends[exp], blk_lo[blk], blk_hi[blk]) - blk_lo[blk], 0)
    exp = jnp.where(real, exp, e_last[n_blocks - 1])
    items = (blk, exp.astype(I32), lo.astype(I32), hi.astype(I32), (k_in == 0).astype(I32))
    return slot_tok.reshape(n_blocks, 1, block), slot_dst.reshape(n_blocks, 1, block), items


def _moe_kernel(blk_ref, exp_ref, lo_ref, hi_ref, first_ref, tok_ref, tokn_ref, dst_ref, x_hbm,
                wg_ref, bg_ref, wu_ref, bu_ref, wd_ref, bd_ref, ys_hbm, xbuf, ybuf, gsem, ssem, *, n_blocks):
    i = pl.program_id(0)
    n_items = pl.num_programs(0)
    bs = xbuf.shape[1]
    b = blk_ref[i]
    lo, hi = lo_ref[i], hi_ref[i]
    first = first_ref[i] > 0
    xs = b % 2
    ysl = i % 2

    def gather(tref, s):
        def body(r, _):
            pltpu.make_async_copy(x_hbm.at[pl.ds(tref[0, r], 1)], xbuf.at[s, pl.ds(r, 1)], gsem.at[s]).start()
            return 0
        lax.fori_loop(0, bs, body, 0, unroll=8)

    def wait_scatter(s, n):
        p = bs
        while p >= 1:
            @pl.when((n & p) != 0)
            def _(p=p):
                pltpu.make_async_copy(ybuf.at[s, pl.ds(0, p)], ys_hbm.at[pl.ds(0, p)], ssem.at[s]).wait()
            p //= 2

    @pl.when(i == 0)
    def _first_block():
        gather(tok_ref, 0)

    @pl.when(first & (b + 1 < n_blocks))
    def _prefetch_next_block():
        gather(tokn_ref, 1 - xs)

    @pl.when(first)
    def _rows_ready():
        pltpu.make_async_copy(x_hbm.at[pl.ds(0, bs)], xbuf.at[xs], gsem.at[xs]).wait()

    @pl.when(i >= 2)
    def _result_buffer_free():
        wait_scatter(ysl, hi_ref[i - 2] - lo_ref[i - 2])

    @pl.when(hi > lo)
    def _compute():
        xb = xbuf[xs].astype(BF16)
        g = jnp.minimum(jnp.dot(xb, wg_ref[...], preferred_element_type=F32) + bg_ref[...], SWIGLU_LIMIT)
        u = jnp.clip(jnp.dot(xb, wu_ref[...], preferred_element_type=F32) + bu_ref[...], -SWIGLU_LIMIT, SWIGLU_LIMIT)
        act = (u + 1.0) * (g * jax.nn.sigmoid(SWIGLU_ALPHA * g))
        ybuf[ysl] = jnp.dot(act.astype(BF16), wd_ref[...], preferred_element_type=F32) + bd_ref[...]

        def scatter(r, _):
            pltpu.make_async_copy(ybuf.at[ysl, pl.ds(r, 1)], ys_hbm.at[pl.ds(dst_ref[0, r], 1)], ssem.at[ysl]).start()
            return 0
        lax.fori_loop(lo, hi, scatter, 0)

    @pl.when(i == n_items - 1)
    def _drain():
        wait_scatter(ysl, hi - lo)

        @pl.when(i >= 1)
        def _():
            wait_scatter(1 - ysl, hi_ref[i - 1] - lo_ref[i - 1])


def _moe(h2, top_idx, wg, bg, wu, bu, wd, bd, block):
    n, d = h2.shape
    dff = wg.shape[2]
    slot_tok, slot_dst, items = _route(top_idx, block)
    n_blocks = slot_tok.shape[0]
    n_items = items[0].shape[0]
    smem_blk = lambda f: pl.BlockSpec((None, 1, block), f, memory_space=pltpu.SMEM)
    wspec = lambda a, b_: pl.BlockSpec((None, a, b_), lambda i, blk, exp, *_: (exp[i], 0, 0))
    grid_spec = pltpu.PrefetchScalarGridSpec(
        num_scalar_prefetch=5,
        grid=(n_items,),
        in_specs=[smem_blk(lambda i, blk, *_: (blk[i], 0, 0)),
                  smem_blk(lambda i, blk, *_: (jnp.minimum(blk[i] + 1, n_blocks - 1), 0, 0)),
                  smem_blk(lambda i, blk, *_: (blk[i], 0, 0)),
                  pl.BlockSpec(memory_space=pl.ANY),
                  wspec(d, dff), wspec(1, dff), wspec(d, dff), wspec(1, dff), wspec(dff, d), wspec(1, d)],
        out_specs=pl.BlockSpec(memory_space=pl.ANY),
        scratch_shapes=[pltpu.VMEM((2, block, d), F32), pltpu.VMEM((2, block, d), F32),
                        pltpu.SemaphoreType.DMA((2,)), pltpu.SemaphoreType.DMA((2,))],
    )
    return pl.pallas_call(
        functools.partial(_moe_kernel, n_blocks=n_blocks),
        grid_spec=grid_spec,
        out_shape=jax.ShapeDtypeStruct((n * TOP_K, d), F32),
        compiler_params=_cparams(("arbitrary",)),
        name="moe_experts",
    )(*items, slot_tok, slot_tok, slot_dst, h2,
      wg, bg.reshape(N_EXPERTS, 1, dff), wu, bu.reshape(N_EXPERTS, 1, dff), wd, bd.reshape(N_EXPERTS, 1, d))


def _combine_kernel(x1_ref, y0_ref, y1_ref, y2_ref, y3_ref, gate_ref, g_ref, y_ref):
    x = x1_ref[...]
    for kk, ys_ref in enumerate((y0_ref, y1_ref, y2_ref, y3_ref)):
        x = x + gate_ref[:, kk:kk + 1] * ys_ref[...]
    r = lax.rsqrt(jnp.mean(x * x, axis=-1, keepdims=True) + EPS)
    y_ref[...] = x * r * g_ref[...]


def _combine(x1, ys, rgate, g_final, tm):
    n, d = x1.shape
    nt = n // tm
    row = lambda w: pl.BlockSpec((tm, w), lambda i: (i, 0))
    ys_specs = [pl.BlockSpec((tm, d), lambda i, j=j: (j * nt + i, 0)) for j in range(TOP_K)]
    return pl.pallas_call(
        _combine_kernel,
        grid=(nt,),
        in_specs=[row(d)] + ys_specs + [row(LANES), pl.BlockSpec((1, d), lambda i: (0, 0))],
        out_specs=row(d),
        out_shape=jax.ShapeDtypeStruct((n, d), F32),
        compiler_params=_cparams(("parallel",)),
        name="combine_norm",
    )(x1, ys, ys, ys, ys, rgate, g_final.reshape(1, d))


def _page_copies(pt_ref, b, cache_hbm, buf, sem, first_page, n_pages):
    return [pltpu.make_async_copy(cache_hbm.at[pt_ref[b, first_page + p]], buf.at[p], sem) for p in range(n_pages)]


def _dsa_sample_select_kernel(pt_ref, iq_ref, iw_ref, ikown_ref, cache_hbm, bias_ref, kbuf, sem, *, topk, page):
    b = pl.program_id(0)
    nb = pl.num_programs(0)
    past = kbuf.shape[2]
    n_pages = past // page
    nk = bias_ref.shape[1]
    slot = b % 2

    def copies(bb, s):
        return [pltpu.make_async_copy(cache_hbm.at[pt_ref[bb, p]], kbuf.at[s, :, p * page:(p + 1) * page], sem.at[s])
                for p in range(n_pages)]

    @pl.when(b == 0)
    def _first():
        for cp in copies(0, 0):
            cp.start()

    @pl.when(b + 1 < nb)
    def _prefetch():
        for cp in copies(b + 1, 1 - slot):
            cp.start()

    for cp in copies(b, slot):
        cp.wait()

    iq = iq_ref[...]
    w = iw_ref[...]
    s = jnp.dot(iq, kbuf[slot].astype(BF16), preferred_element_type=F32)
    score = jnp.sum(w * jnp.maximum(s, 0.0), axis=0, keepdims=True)
    own = ikown_ref[...].astype(BF16).astype(F32)
    s_own = jnp.sum(iq.astype(F32) * own, axis=1, keepdims=True)
    score_own = jnp.sum(w * jnp.maximum(s_own, 0.0), axis=0, keepdims=True)
    tail_lane = lax.broadcasted_iota(I32, (1, nk - past), 1)
    key = jnp.concatenate([_sort_key(score),
                           jnp.where(tail_lane == 0, _sort_key(score_own), KEY_MASKED)], axis=1)
    idx = lax.broadcasted_iota(I32, (1, nk), 1)

    def count_ge(cand):
        return jnp.sum((key >= cand).astype(I32), axis=1, keepdims=True)

    t_k, _ = _kth_largest_key(count_ge, nk, topk, (1, 1))
    need = topk - count_ge(t_k + 1)
    eq = key == t_k
    nbits = int(nk).bit_length()

    def idx_step(i, jb):
        cand = jb + (jnp.int32(1) << (nbits - 1 - i))
        c = jnp.sum((eq & (idx < cand)).astype(I32), axis=1, keepdims=True)
        return jnp.where(c <= need, cand, jb)

    j_star = lax.fori_loop(0, nbits, idx_step, jnp.zeros((1, 1), I32))
    sel = ((key > t_k) | (eq & (idx < j_star))) & (key > KEY_NEG_INF)
    bias_ref[...] = jnp.where(sel, 0.0, -jnp.inf).astype(F32)


def _dsa_sample_attend_kernel(pt_ref, q_ref, kown_ref, vown_ref, bias_ref, bias_own_ref, ck_hbm, cv_hbm, o_ref,
                              kbuf, vbuf, ksem, vsem, acc_ref, m_ref, l_ref):
    b = pl.program_id(0)
    c = pl.program_id(1)
    nb, ncg = pl.num_programs(0), pl.num_programs(1)
    pg, page = kbuf.shape[1], kbuf.shape[4]
    step = b * ncg + c
    slot = step % 2

    def copies(bb, cc, s):
        return (_page_copies(pt_ref, bb, ck_hbm, kbuf.at[s], ksem.at[s], cc * pg, pg)
                + _page_copies(pt_ref, bb, cv_hbm, vbuf.at[s], vsem.at[s], cc * pg, pg))

    @pl.when(step == 0)
    def _first():
        for cp in copies(0, 0, 0):
            cp.start()

    @pl.when(step + 1 < nb * ncg)
    def _prefetch():
        nxt = step + 1
        for cp in copies(nxt // ncg, nxt % ncg, 1 - slot):
            cp.start()

    @pl.when(c == 0)
    def _init():
        acc_ref[...] = jnp.zeros_like(acc_ref)
        m_ref[...] = jnp.full(m_ref.shape, -1e30, F32)
        l_ref[...] = jnp.zeros_like(l_ref)

    for cp in copies(b, c, slot):
        cp.wait()

    for h in range(ATT_HEADS):
        qc = q_ref[h]
        s = [jnp.sum(kbuf[slot, g, h] * qc, axis=0, keepdims=True) + bias_ref[:, g * page:(g + 1) * page]
             for g in range(pg)]
        top = functools.reduce(jnp.maximum, s)
        m_old = m_ref[h]
        m_new = jnp.maximum(m_old, jnp.max(top, axis=1, keepdims=True))
        alpha = jnp.exp2(m_old - m_new)
        acc = alpha * acc_ref[h]
        lsum = alpha * l_ref[h]
        for g in range(pg):
            p = jnp.exp2(s[g] - m_new)
            lsum = lsum + p
            acc = acc + p * vbuf[slot, g, h]
        acc_ref[h] = acc
        l_ref[h] = lsum
        m_ref[h] = m_new

    @pl.when(c == ncg - 1)
    def _finish():
        for h in range(ATT_HEADS):
            s_own = jnp.sum(q_ref[h] * kown_ref[h], axis=0, keepdims=True) + bias_own_ref[:, 0:1]
            m_old = m_ref[h]
            m_new = jnp.maximum(m_old, s_own)
            alpha = jnp.exp2(m_old - m_new)
            p_own = jnp.exp2(s_own - m_new)[:, 0:1]
            l_tot = jnp.sum(alpha * l_ref[h], axis=1, keepdims=True) + p_own
            o = jnp.sum(alpha * acc_ref[h], axis=1, keepdims=True) + p_own * vown_ref[h]
            o_ref[h] = o / l_tot


def _dsa_sample(q, k_own, v_own, iq, ik_own, iw, cache_k, cache_v, cache_ik, page_table, pages_per_step):
    bsz = q.shape[0]
    n_pool, page = cache_ik.shape[:2]
    n_pages = page_table.shape[1]
    past = n_pages * page
    topk = min(TOPK_MAX, (past + 1) // 4)
    nk = past + LANES
    assert n_pages % pages_per_step == 0
    blk3 = lambda s: pl.BlockSpec((None,) + s, lambda b, *_: (b, 0, 0))
    blk4 = lambda s: pl.BlockSpec((None,) + s, lambda b, *_: (b, 0, 0, 0))
    bias = pl.pallas_call(
        functools.partial(_dsa_sample_select_kernel, topk=topk, page=page),
        grid_spec=pltpu.PrefetchScalarGridSpec(
            num_scalar_prefetch=1, grid=(bsz,),
            in_specs=[blk3((IDX_HEADS, IDX_DIM)), blk3((IDX_HEADS, 1)), blk3((1, IDX_DIM)),
                      pl.BlockSpec(memory_space=pl.ANY)],
            out_specs=blk3((1, nk)),
            scratch_shapes=[pltpu.VMEM((2, IDX_DIM, past), F32), pltpu.SemaphoreType.DMA((2,))]),
        out_shape=jax.ShapeDtypeStruct((bsz, 1, nk), F32),
        compiler_params=_cparams(("arbitrary",)),
        name="dsa_sample_select",
    )(page_table, iq.reshape(bsz, IDX_HEADS, IDX_DIM), iw.reshape(bsz, IDX_HEADS, 1), ik_own.reshape(bsz, 1, IDX_DIM),
      jnp.transpose(cache_ik, (0, 2, 1)))
    ncg = n_pages // pages_per_step
    keys_per_step = pages_per_step * page
    bias_past = bias[:, :, :past].reshape(bsz, ncg, 1, keys_per_step)
    bias_own = bias[:, :, past:]
    col = lambda a: a.astype(F32).reshape(bsz, ATT_HEADS, ATT_DIM, 1)
    pages = pltpu.VMEM((2, pages_per_step, ATT_HEADS, ATT_DIM, page), F32)
    att = pl.pallas_call(
        _dsa_sample_attend_kernel,
        grid_spec=pltpu.PrefetchScalarGridSpec(
            num_scalar_prefetch=1, grid=(bsz, ncg),
            in_specs=[blk4((ATT_HEADS, ATT_DIM, 1)), blk4((ATT_HEADS, ATT_DIM, 1)), blk4((ATT_HEADS, ATT_DIM, 1)),
                      pl.BlockSpec((None, None, 1, keys_per_step), lambda b, c, *_: (b, c, 0, 0)), blk3((1, LANES)),
                      pl.BlockSpec(memory_space=pl.ANY), pl.BlockSpec(memory_space=pl.ANY)],
            out_specs=blk4((ATT_HEADS, ATT_DIM, 1)),
            scratch_shapes=[pages, pages, pltpu.SemaphoreType.DMA((2,)), pltpu.SemaphoreType.DMA((2,)),
                            pltpu.VMEM((ATT_HEADS, ATT_DIM, page), F32), pltpu.VMEM((ATT_HEADS, 1, page), F32),
                            pltpu.VMEM((ATT_HEADS, 1, page), F32)]),
        out_shape=jax.ShapeDtypeStruct((bsz, ATT_HEADS, ATT_DIM, 1), F32),
        compiler_params=_cparams(("arbitrary", "arbitrary")),
        name="dsa_sample_attend",
    )(page_table, col(q), col(k_own), col(v_own), bias_past, bias_own,
      jnp.transpose(cache_k, (0, 2, 3, 1)), jnp.transpose(cache_v, (0, 2, 3, 1)))
    return att.reshape(bsz, ATT_W).astype(BF16)


def _mlstm_step_kernel(q_ref, k_ref, v_ref, mo_ref, g_ref, gate_ref, gbias_ref, c_ref, n_ref, m_ref,
                       hm_ref, c_out_ref, n_out_ref, m_out_ref):
    gates = _softcap(gate_ref[...] + gbias_ref[...])
    lane = lax.broadcasted_iota(I32, (1, LANES), 1)
    m_out = jnp.zeros((1, LANES), F32)
    for h in range(ML_HEADS):
        ig = gates[:, M_MI + h:M_MI + h + 1]
        lf = _log_sigmoid(gates[:, M_MF + h:M_MF + h + 1])
        m_prev = m_ref[:, h:h + 1]
        q = q_ref[:, h * ML_QK:(h + 1) * ML_QK].astype(F32)
        k = k_ref[:, h * ML_QK:(h + 1) * ML_QK].astype(F32)
        v = v_ref[h].astype(F32)
        c = c_ref[h]
        n = n_ref[h:h + 1, :]
        m_inter = lf + m_prev
        m_t = jnp.maximum(m_inter, ig)
        w_intra = jnp.exp(ig - m_t)
        w_inter = jnp.exp(m_inter - m_t)
        qk = jnp.sum(q * k, axis=1, keepdims=True) * w_intra
        num = w_inter * jnp.sum(c * q, axis=1, keepdims=True) + qk * v
        den = w_inter * jnp.sum(n * q, axis=1, keepdims=True) + qk
        hh = num / jnp.maximum(jnp.abs(den), jnp.exp(-m_t))
        hh = hh * lax.rsqrt(jnp.mean(hh * hh, axis=0, keepdims=True) + EPS)
        hm_ref[h] = (hh * g_ref[h] * jax.nn.sigmoid(mo_ref[h])).astype(hm_ref.dtype)
        decay = jnp.exp(lf + m_prev - m_t)
        w_s = jnp.exp(ig - m_t)
        c_out_ref[h] = decay * c + (w_s * v) * k
        n_out_ref[h:h + 1, :] = decay * n + w_s * k
        m_out = jnp.where(lane == h, m_t, m_out)
    m_out_ref[...] = m_out


def _mlstm_step(mq, mk, mv, mo, misc, b_igate, b_fgate, g_mlstm, state_c, state_n, state_m):
    bsz = mq.shape[0]
    _, brow = _gate_bias(b_igate, b_fgate)
    b3 = lambda s: pl.BlockSpec((None,) + s, lambda b: (b,) + (0,) * len(s))
    cst = lambda s: pl.BlockSpec(s, lambda b: (0,) * len(s))
    hm, c_new, n_new, m_new = pl.pallas_call(
        _mlstm_step_kernel,
        grid=(bsz,),
        in_specs=[b3((1, ML_QW)), b3((1, ML_QW)), b3((ML_HEADS, ML_V, 1)), b3((ML_HEADS, ML_V, 1)), cst((ML_HEADS, ML_V, 1)),
                  b3((1, LANES)), cst((1, LANES)), b3((ML_HEADS, ML_V, ML_QK)), b3((ML_HEADS, ML_QK)), b3((1, ML_HEADS))],
        out_specs=[b3((ML_HEADS, ML_V, 1)), b3((ML_HEADS, ML_V, ML_QK)), b3((ML_HEADS, ML_QK)), b3((1, LANES))],
        out_shape=[jax.ShapeDtypeStruct((bsz, ML_HEADS, ML_V, 1), F32),
                   jax.ShapeDtypeStruct((bsz, ML_HEADS, ML_V, ML_QK), F32),
                   jax.ShapeDtypeStruct((bsz, ML_HEADS, ML_QK), F32),
                   jax.ShapeDtypeStruct((bsz, 1, LANES), F32)],
        compiler_params=_cparams(("parallel",)),
        name="mlstm_step",
    )(mq.reshape(bsz, 1, ML_QW), mk.reshape(bsz, 1, ML_QW), mv.astype(F32).reshape(bsz, ML_HEADS, ML_V, 1),
      mo.reshape(bsz, ML_HEADS, ML_V, 1), g_mlstm.astype(F32).reshape(ML_HEADS, ML_V, 1),
      misc.reshape(bsz, 1, LANES), brow, state_c, state_n, state_m.reshape(bsz, 1, ML_HEADS))
    return hm.reshape(bsz, ML_W).astype(BF16), c_new, n_new, m_new[:, 0, :ML_HEADS]


TQ = 512
TM_PROJ = 512
ML_CHUNK = 256
MOE_BLOCK_PROMPT = 512
MOE_BLOCK_SAMPLE = 128
SAMPLE_PAGES_PER_STEP = 8


def kernel(x_prompt, x_sample, cache_k, cache_v, cache_idx_k, state_C, state_n, state_m, page_table, g_mix, w_in, b_igate, b_fgate, g_mlstm, w_out, g_ffn, w_router, b_router, w_gate, b_gate, w_up, b_up, w_down, b_down, g_final):
    bp, tp, d = x_prompt.shape
    bs, ts, _ = x_sample.shape
    assert w_in.shape[0] == 1 and ts == 1, "one layer, one new token per sampled sequence"
    page = cache_k.shape[2]
    n_pages = page_table.shape[1]
    past = n_pages * page

    w = _build_w_in(w_in[0])
    wo = w_out[0].astype(BF16)
    wr, br = _router_params(w_router[0], b_router[0])
    moe_w = (w_gate[0].astype(BF16), b_gate[0], w_up[0].astype(BF16), b_up[0], w_down[0].astype(BF16), b_down[0])

    def tail(x2, att, hm, tm, moe_block):
        x1, h2, ridx, rgate = _outproj(x2, att, hm, wo, g_ffn[0], wr, br, tm)
        ys = _moe(h2, ridx[:, :TOP_K], *moe_w, moe_block)
        return _combine(x1, ys, rgate, g_final, tm)

    xp2 = x_prompt.reshape(bp * tp, d)
    cos_p, sin_p = _rope_tables(jnp.arange(tp))
    q, kf, kb, vf, vb, iq, misc, mq, mk, mv, mo = _inproj(xp2, g_mix[0], w, cos_p, sin_p, TM_PROJ)
    ik_p = misc[:, M_IK:M_IK + IDX_DIM]
    att = _dsa_prompt(q, kb, vb, iq, ik_p.astype(BF16), misc[:, M_IW:M_IW + IDX_HEADS], bp, tp, TQ)
    hm, c_p, n_p, m_p = _mlstm_prompt(mq, mk, mv, mo, misc, b_igate[0], b_fgate[0], g_mlstm[0], bp, tp, ML_CHUNK)
    y_prompt = tail(xp2, att, hm, TM_PROJ, MOE_BLOCK_PROMPT).reshape(bp, tp, d)

    xs2 = x_sample.reshape(bs, d)
    cos_s, sin_s = _rope_tables(jnp.full((bs,), past))
    q_s, kf_s, kb_s, vf_s, vb_s, iq_s, misc_s, mq_s, mk_s, mv_s, mo_s = _inproj(xs2, g_mix[0], w, cos_s, sin_s, bs)
    ik_s = misc_s[:, M_IK:M_IK + IDX_DIM]
    att_s = _dsa_sample(q_s, kf_s, vf_s, iq_s, ik_s, misc_s[:, M_IW:M_IW + IDX_HEADS],
                        cache_k[0], cache_v[0], cache_idx_k[0], page_table, math.gcd(SAMPLE_PAGES_PER_STEP, n_pages))
    hm_s, c_s, n_s, m_s = _mlstm_step(mq_s, mk_s, mv_s, mo_s, misc_s, b_igate[0], b_fgate[0], g_mlstm[0],
                                      state_C[0], state_n[0], state_m[0])
    y_sample = tail(xs2, att_s, hm_s, bs, min(MOE_BLOCK_SAMPLE, bs * TOP_K)).reshape(bs, ts, d)

    return (y_prompt, y_sample,
            kf.reshape(1, bp, tp // page, page, ATT_HEADS, ATT_DIM), vf.reshape(1, bp, tp // page, page, ATT_HEADS, ATT_DIM),
            ik_p.reshape(1, bp, tp // page, page, IDX_DIM), c_p[None], n_p[None], m_p[None],
            kf_s.reshape(1, bs, ts, ATT_HEADS, ATT_DIM), vf_s.reshape(1, bs, ts, ATT_HEADS, ATT_DIM),
            ik_s.reshape(1, bs, ts, IDX_DIM), c_s[None], n_s[None], m_s[None])
```

```python
import functools
import math

import jax
import jax.numpy as jnp
import numpy as np
from jax import lax
from jax.experimental import pallas as pl
from jax.experimental.pallas import tpu as pltpu

F32 = jnp.float32
BF16 = jnp.bfloat16
I32 = jnp.int32

ATT_HEADS = 8
ATT_DIM = 64
IDX_HEADS = 8
IDX_DIM = 64
TOPK_MAX = 256
ML_HEADS = 4
ML_QK = 64
ML_V = 128
GATE_CAP = 15.0
N_EXPERTS = 32
TOP_K = 4
SWIGLU_LIMIT = 7.0
SWIGLU_ALPHA = 1.702
ROPE_THETA = 10000.0
EPS = 1e-5

ATT_W = ATT_HEADS * ATT_DIM
IDX_W = IDX_HEADS * IDX_DIM
ML_QW = ML_HEADS * ML_QK
ML_W = ML_HEADS * ML_V

LANES = 128
Q_SCALE = (ATT_DIM ** -0.5) * math.log2(math.e)
VMEM_LIMIT = 56 * 1024 * 1024

KEY_MASKED = -(2 ** 31)
KEY_NEG_INF = int(np.int32(np.uint32(0xFF800000)) ^ np.int32(0x7FFFFFFF))

C_AQ, C_AK, C_AV, C_IQ, C_MQ, C_MK, C_MV, C_MO, C_MISC = 0, 512, 1024, 1536, 2048, 2304, 2560, 3072, 3584
C_AQR, C_AKR, C_IQR, C_MISCR, C_END = 3712, 4224, 4736, 5248, 5376
M_IK, M_IW, M_MI, M_MF = 0, 64, 72, 76


def _cparams(sem, vmem=VMEM_LIMIT):
    return pltpu.CompilerParams(dimension_semantics=sem, vmem_limit_bytes=vmem)


def _rot_cols(w):
    d, n = w.shape
    w4 = w.reshape(d, n // 64, 2, 32)
    return jnp.stack([w4[:, :, 1, :], w4[:, :, 0, :]], axis=2).reshape(d, n)


def _build_w_in(w_in):
    d = w_in.shape[0]
    sp = np.cumsum([0, ATT_W, ATT_W, ATT_W, IDX_W, IDX_HEADS, IDX_DIM, ML_QW, ML_QW, ML_W, ML_W, ML_HEADS, ML_HEADS])
    aq, ak, av, iq, iw, ik, mq, mk, mv, mo, mi, mf = [w_in[:, sp[i]:sp[i + 1]] for i in range(12)]
    misc = jnp.concatenate([ik, iw, mi, mf, jnp.zeros((d, LANES - 80), w_in.dtype)], axis=1)
    miscr = jnp.concatenate([_rot_cols(ik), jnp.zeros((d, LANES - 64), w_in.dtype)], axis=1)
    w = jnp.concatenate([aq, ak, av, iq, mq, mk, mv, mo, misc, _rot_cols(aq), _rot_cols(ak), _rot_cols(iq), miscr], axis=1)
    assert w.shape[1] == C_END
    return w.astype(BF16)


def _rope_tables(pos):
    half = ATT_DIM // 2
    inv = ROPE_THETA ** (-jnp.arange(half, dtype=F32) / half)
    ang = pos.astype(F32)[:, None] * inv[None, :]
    cos, sin = jnp.cos(ang), jnp.sin(ang)
    cos_t = jnp.concatenate([cos, cos, cos, cos], axis=1)
    sin_t = jnp.concatenate([-sin, sin, -sin, sin], axis=1)
    return cos_t, sin_t


def _inproj_kernel(x_ref, g_ref, w_ref, cos_ref, sin_ref,
                   q_ref, kf_ref, kb_ref, vf_ref, vb_ref, iq_ref, misc_ref, mq_ref, mk_ref, mv_ref, mo_ref):
    x = x_ref[...]
    r = lax.rsqrt(jnp.mean(x * x, axis=-1, keepdims=True) + EPS)
    h = (x * r * g_ref[...]).astype(BF16)
    cos = cos_ref[...]
    sin = sin_ref[...]

    def dot(c0, n):
        return jnp.dot(h, w_ref[:, c0:c0 + n], preferred_element_type=F32)

    def rope(c0, cr, n, lane_cos, lane_sin):
        z = dot(c0, n)
        zr = dot(cr, n)
        return [z[:, j:j + LANES] * lane_cos + zr[:, j:j + LANES] * lane_sin for j in range(0, n, LANES)]

    for j, t in enumerate(rope(C_AQ, C_AQR, ATT_W, cos, sin)):
        q_ref[:, j * LANES:(j + 1) * LANES] = (t * Q_SCALE).astype(BF16)
    for j, t in enumerate(rope(C_AK, C_AKR, ATT_W, cos, sin)):
        kf_ref[:, j * LANES:(j + 1) * LANES] = t
        kb_ref[:, j * LANES:(j + 1) * LANES] = t.astype(BF16)
    v = dot(C_AV, ATT_W)
    vf_ref[...] = v
    vb_ref[...] = v.astype(BF16)
    for j, t in enumerate(rope(C_IQ, C_IQR, IDX_W, cos, sin)):
        iq_ref[:, j * LANES:(j + 1) * LANES] = t.astype(BF16)
    lane = lax.broadcasted_iota(I32, cos.shape, 1)
    mcos = jnp.where(lane < IDX_DIM, cos, 1.0)
    msin = jnp.where(lane < IDX_DIM, sin, 0.0)
    misc_ref[...] = rope(C_MISC, C_MISCR, LANES, mcos, msin)[0]
    mq_ref[...] = dot(C_MQ, ML_QW).astype(BF16)
    mk_ref[...] = (dot(C_MK, ML_QW) * (ML_QK ** -0.5)).astype(BF16)
    mv_ref[...] = dot(C_MV, ML_W).astype(BF16)
    mo_ref[...] = dot(C_MO, ML_W)


def _inproj(x2, g, w, cos_t, sin_t, tm):
    n, d = x2.shape
    tt = cos_t.shape[0]
    assert n % tm == 0 and tt % tm == 0
    nt = tt // tm
    row = lambda i: (i, 0)
    outs = [(ATT_W, BF16), (ATT_W, F32), (ATT_W, BF16), (ATT_W, F32), (ATT_W, BF16), (IDX_W, BF16), (LANES, F32),
            (ML_QW, BF16), (ML_QW, BF16), (ML_W, BF16), (ML_W, F32)]
    return pl.pallas_call(
        _inproj_kernel,
        grid=(n // tm,),
        in_specs=[pl.BlockSpec((tm, d), row),
                  pl.BlockSpec((1, d), lambda i: (0, 0)),
                  pl.BlockSpec((d, C_END), lambda i: (0, 0), pipeline_mode=pl.Buffered(1)),
                  pl.BlockSpec((tm, LANES), lambda i: (i % nt, 0)),
                  pl.BlockSpec((tm, LANES), lambda i: (i % nt, 0))],
        out_specs=[pl.BlockSpec((tm, c), row) for c, _ in outs],
        out_shape=[jax.ShapeDtypeStruct((n, c), dt) for c, dt in outs],
        compiler_params=_cparams(("parallel",)),
        name="inproj",
    )(x2, g.reshape(1, d), w, cos_t, sin_t)


def _sort_key(score):
    bits = lax.bitcast_convert_type(score, I32)
    return bits ^ ((bits >> 31) & 0x7FFFFFFF)


BITS_PER_CHECK = 4


def _kth_largest_key(count_ge, total, k, shape):
    def pending(cnt):
        return jnp.max((cnt != k).astype(I32))

    t0 = jnp.full(shape, KEY_MASKED, I32)
    cnt0 = jnp.full(shape, total, I32)
    i0 = jnp.int32(0)

    def bit_steps(carry):
        i, _, t, cnt = carry
        for _ in range(BITS_PER_CHECK):
            cand = t + (jnp.int32(1) << (31 - i))
            c = count_ge(cand)
            ok = c >= k
            t = jnp.where(ok, cand, t)
            cnt = jnp.where(ok, c, cnt)
            i = i + 1
        return i, pending(cnt), t, cnt

    _, _, t, cnt = lax.while_loop(lambda c: (c[0] < 32) & (c[1] > 0), bit_steps, (i0, pending(cnt0), t0, cnt0))
    return t, cnt


def _dsa_prompt_kernel(qT_ref, iqT_ref, iwT_ref, kc_ref, vT_ref, ikc_ref, oT_ref,
                       keys_ref, qpad_ref, bias_ref, *head_refs, topk):
    j = pl.program_id(1)
    nck = j + 1
    tq = keys_ref.shape[1]
    kc_sz = kc_ref.shape[1]
    assert kc_sz == tq

    def chunk_rows(c):
        return pl.ds(pl.multiple_of(c * kc_sz, kc_sz), kc_sz)

    row = lax.broadcasted_iota(I32, (kc_sz, tq), 0)
    col = lax.broadcasted_iota(I32, (kc_sz, tq), 1)

    def score_chunk(c, _):
        ik_c = ikc_ref[c]
        acc = jnp.zeros((kc_sz, tq), F32)
        for h in range(IDX_HEADS):
            s = jnp.dot(ik_c, iqT_ref[h * IDX_DIM:(h + 1) * IDX_DIM, :], preferred_element_type=F32)
            acc = acc + iwT_ref[h:h + 1, :] * jnp.maximum(s, 0.0)
        admissible = (c < j) | (row <= col)
        keys_ref[chunk_rows(c), :] = jnp.where(admissible, _sort_key(acc), KEY_MASKED)
        return 0

    lax.fori_loop(0, nck, score_chunk, 0)

    def count_ge(cand):
        def body(c, acc):
            ge = (keys_ref[chunk_rows(c), :] >= cand).astype(I32)
            return acc + ge.reshape(kc_sz // 8, 8, tq).sum(axis=0)
        acc = lax.fori_loop(0, nck, body, jnp.zeros((8, tq), I32))
        return acc.sum(axis=0, keepdims=True)

    t_k, cnt = _kth_largest_key(count_ge, nck * kc_sz, topk, (1, tq))
    ties = (cnt > topk) & (t_k > KEY_NEG_INF)

    @pl.when(jnp.max(ties.astype(I32)) > 0)
    def _drop_late_ties():
        need = (topk - count_ge(t_k + 1)).astype(F32)
        ltri = (col < row).astype(BF16)

        def body(c, carry):
            kc = keys_ref[chunk_rows(c), :]
            eq = (kc == t_k) & ties
            eqf = jnp.where(eq, 1.0, 0.0).astype(BF16)
            rank = jnp.dot(ltri, eqf, preferred_element_type=F32) + carry
            keys_ref[chunk_rows(c), :] = jnp.where(eq & (rank >= need), KEY_MASKED, kc)
            return carry + jnp.sum(eqf.astype(F32), axis=0, keepdims=True)

        lax.fori_loop(0, nck, body, jnp.zeros((1, tq), F32))

    t_sel = jnp.maximum(t_k, KEY_NEG_INF + 1)

    hrow = lax.broadcasted_iota(I32, (LANES, tq), 0)
    for h in range(ATT_HEADS):
        pair = qT_ref[(h // 2) * LANES:(h // 2 + 1) * LANES, :].astype(F32)
        mine = (hrow >= (h % 2) * ATT_DIM) & (hrow < (h % 2 + 1) * ATT_DIM)
        qpad_ref[h] = jnp.where(mine, pair, 0.0).astype(BF16)
    accs, ms, ls = head_refs[:ATT_HEADS], head_refs[ATT_HEADS:2 * ATT_HEADS], head_refs[2 * ATT_HEADS:]
    for h in range(ATT_HEADS):
        accs[h][...] = jnp.zeros_like(accs[h])
        ms[h][...] = jnp.full(ms[h].shape, -1e30, F32)
        ls[h][...] = jnp.zeros_like(ls[h])

    def attend_chunk(c, _):
        bias_ref[...] = jnp.where(keys_ref[chunk_rows(c), :] >= t_sel, 0.0, -jnp.inf).astype(F32)

        def qk(h):
            kp = kc_ref[c, :, (h // 2) * LANES:(h // 2 + 1) * LANES]
            return jnp.dot(kp, qpad_ref[h], preferred_element_type=F32)

        s_next = qk(0)
        for h in range(ATT_HEADS):
            s = s_next + bias_ref[...]
            if h + 1 < ATT_HEADS:
                s_next = qk(h + 1)
            m_old = ms[h][...]
            m_new = jnp.maximum(m_old, jnp.max(s, axis=0, keepdims=True))
            alpha = jnp.exp2(m_old - m_new)
            p = jnp.exp2(s - m_new)
            ls[h][...] = alpha * ls[h][...] + jnp.sum(p, axis=0, keepdims=True)
            ms[h][...] = m_new
            pv = jnp.dot(vT_ref[c, h * ATT_DIM:(h + 1) * ATT_DIM, :], p.astype(BF16), preferred_element_type=F32)
            accs[h][...] = alpha * accs[h][...] + pv
        return 0

    lax.fori_loop(0, nck, attend_chunk, 0)
    for h in range(ATT_HEADS):
        oT_ref[h * ATT_DIM:(h + 1) * ATT_DIM, :] = (accs[h][...] / ls[h][...]).astype(oT_ref.dtype)


def _dsa_prompt(q, k, v, iq, ik, iw, bsz, t_len, tq):
    assert t_len % tq == 0
    nq = t_len // tq
    topk = min(TOPK_MAX, t_len // 4)

    def to_t(a):
        return a.reshape(bsz, nq, tq, a.shape[-1]).swapaxes(2, 3)

    qT, iqT, iwT, vT = to_t(q), to_t(iq), to_t(iw), to_t(v)
    kc = k.reshape(bsz, nq, tq, ATT_W)
    ikc = ik.reshape(bsz, nq, tq, IDX_DIM)
    per_q = lambda c: pl.BlockSpec((None, None, c, tq), lambda b, j: (b, j, 0, 0))
    per_b = lambda s: pl.BlockSpec((None,) + s, lambda b, j: (b, 0, 0, 0), pipeline_mode=pl.Buffered(1))
    oT = pl.pallas_call(
        functools.partial(_dsa_prompt_kernel, topk=topk),
        grid=(bsz, nq),
        in_specs=[per_q(ATT_W), per_q(IDX_W), per_q(IDX_HEADS),
                  per_b((nq, tq, ATT_W)), per_b((nq, ATT_W, tq)), per_b((nq, tq, IDX_DIM))],
        out_specs=per_q(ATT_W),
        out_shape=jax.ShapeDtypeStruct((bsz, nq, ATT_W, tq), BF16),
        scratch_shapes=([pltpu.VMEM((t_len, tq), I32), pltpu.VMEM((ATT_HEADS, LANES, tq), BF16), pltpu.VMEM((tq, tq), F32)]
                        + [pltpu.VMEM((ATT_DIM, tq), F32)] * ATT_HEADS + [pltpu.VMEM((1, tq), F32)] * (2 * ATT_HEADS)),
        compiler_params=_cparams(("parallel", "arbitrary")),
        name="dsa_prompt",
    )(qT, iqT, iwT, kc, vT, ikc)
    return oT.swapaxes(2, 3).reshape(bsz * t_len, ATT_W)


def _softcap(a):
    return GATE_CAP * jnp.tanh(a / GATE_CAP)


def _log_sigmoid(a):
    return -(jnp.maximum(-a, 0.0) + jnp.log1p(jnp.exp(-jnp.abs(a))))


def _dot_f32(a, b):
    return jnp.dot(a, b, precision=lax.Precision.HIGHEST, preferred_element_type=F32)


def _mlstm_kernel(q_ref, kT_ref, v_ref, mo_ref, gcol_ref, grow_ref, bcol_ref, brow_ref, g_ref,
                  hm_ref, s_out_ref, m_out_ref, s_ref, m_ref):
    c = pl.program_id(1)
    L = q_ref.shape[0]

    @pl.when(c == 0)
    def _init():
        s_ref[...] = jnp.zeros_like(s_ref)
        m_ref[...] = jnp.zeros_like(m_ref)

    row = lax.broadcasted_iota(I32, (L, L), 0)
    col = lax.broadcasted_iota(I32, (L, L), 1)
    causal = col <= row
    tri_l = causal.astype(F32)
    tri_u = (row <= col).astype(F32)

    gc = _softcap(gcol_ref[...] + brow_ref[...])
    b_col = _dot_f32(tri_l, _log_sigmoid(gc))
    gr = _softcap(grow_ref[...] + bcol_ref[...])
    b_row = _dot_f32(_log_sigmoid(gr), tri_u)
    ones_col = (lax.broadcasted_iota(I32, (L, ML_V), 1) == 0).astype(BF16)

    for h in range(ML_HEADS):
        bc = b_col[:, M_MF + h:M_MF + h + 1]
        br = b_row[ML_HEADS + h:ML_HEADS + h + 1, :]
        ir = gr[h:h + 1, :]
        m_prev = m_ref[h:h + 1, 0:1]
        dmat = jnp.where(causal, bc - br + ir, -jnp.inf)
        m_inter = bc + m_prev
        m_t = jnp.maximum(m_inter, jnp.max(dmat, axis=1, keepdims=True))
        w_intra = jnp.exp(dmat - m_t)
        w_inter = jnp.exp(m_inter - m_t)
        qh = q_ref[:, h * ML_QK:(h + 1) * ML_QK]
        kth = kT_ref[h * ML_QK:(h + 1) * ML_QK, :]
        vext = jnp.concatenate([v_ref[:, h * ML_V:(h + 1) * ML_V], ones_col], axis=1)
        s_h = s_ref[h]
        qk = jnp.dot(qh, kth, preferred_element_type=F32) * w_intra
        hext = (w_inter * jnp.dot(qh, s_h.astype(BF16), preferred_element_type=F32)
                + jnp.dot(qk.astype(BF16), vext, preferred_element_type=F32))
        num = hext[:, :ML_V]
        den = hext[:, ML_V:ML_V + 1]
        hh = num / jnp.maximum(jnp.abs(den), jnp.exp(-m_t))
        hh = hh * lax.rsqrt(jnp.mean(hh * hh, axis=-1, keepdims=True) + EPS)
        hs = slice(h * ML_V, (h + 1) * ML_V)
        hm_ref[:, hs] = (hh * g_ref[:, hs] * jax.nn.sigmoid(mo_ref[:, hs])).astype(hm_ref.dtype)
        b_last = bc[L - 1:L, :]
        m_new = m_t[L - 1:L, :]
        decay = jnp.exp(b_last + m_prev - m_new)
        w_s = jnp.exp(b_last - br + ir - m_new)
        kw = (kth.astype(F32) * w_s).astype(BF16)
        s_ref[h] = decay * s_h + jnp.dot(kw, vext, preferred_element_type=F32)
        m_ref[h:h + 1, :] = jnp.broadcast_to(m_new, (1, LANES))

    @pl.when(c == pl.num_programs(1) - 1)
    def _emit_state():
        s_out_ref[...] = s_ref[...]
        m_out_ref[...] = m_ref[...]


def _gate_bias(b_igate, b_fgate):
    bias8 = jnp.concatenate([b_igate, b_fgate]).astype(F32)
    brow = jnp.zeros((1, LANES), F32).at[0, M_MI:M_MI + 2 * ML_HEADS].set(bias8)
    return bias8.reshape(2 * ML_HEADS, 1), brow


def _mlstm_prompt(mq, mk, mv, mo, misc, b_igate, b_fgate, g_mlstm, bsz, t_len, L):
    assert t_len % L == 0
    nc = t_len // L
    kT = mk.reshape(bsz, nc, L, ML_QW).swapaxes(2, 3)
    grow = misc[:, M_MI:M_MI + 2 * ML_HEADS].reshape(bsz, nc, L, 2 * ML_HEADS).swapaxes(2, 3)
    bcol, brow = _gate_bias(b_igate, b_fgate)
    rows = lambda w: pl.BlockSpec((L, w), lambda b, c: (b * nc + c, 0))
    cst = lambda s: pl.BlockSpec(s, lambda b, c: (0,) * len(s))
    hm, s_out, m_out = pl.pallas_call(
        _mlstm_kernel,
        grid=(bsz, nc),
        in_specs=[rows(ML_QW), pl.BlockSpec((None, None, ML_QW, L), lambda b, c: (b, c, 0, 0)), rows(ML_W), rows(ML_W),
                  rows(LANES), pl.BlockSpec((None, None, 2 * ML_HEADS, L), lambda b, c: (b, c, 0, 0)),
                  cst((2 * ML_HEADS, 1)), cst((1, LANES)), cst((1, ML_W))],
        out_specs=[rows(ML_W), pl.BlockSpec((None, ML_HEADS, ML_QK, 2 * ML_V), lambda b, c: (b, 0, 0, 0)),
                   pl.BlockSpec((None, 2 * ML_HEADS, LANES), lambda b, c: (b, 0, 0))],
        out_shape=[jax.ShapeDtypeStruct((bsz * t_len, ML_W), BF16),
                   jax.ShapeDtypeStruct((bsz, ML_HEADS, ML_QK, 2 * ML_V), F32),
                   jax.ShapeDtypeStruct((bsz, 2 * ML_HEADS, LANES), F32)],
        scratch_shapes=[pltpu.VMEM((ML_HEADS, ML_QK, 2 * ML_V), F32), pltpu.VMEM((2 * ML_HEADS, LANES), F32)],
        compiler_params=_cparams(("parallel", "arbitrary")),
        name="mlstm_prompt",
    )(mq, kT, mv, mo, misc, grow, bcol, brow, g_mlstm.reshape(1, ML_W))
    c_state = s_out[:, :, :, :ML_V].swapaxes(2, 3)
    n_state = s_out[:, :, :, ML_V]
    m_state = m_out[:, :ML_HEADS, 0]
    return hm, c_state, n_state, m_state


def _outproj_kernel(x_ref, att_ref, hm_ref, wo_ref, g_ref, wr_ref, br_ref, x1_ref, h2_ref, ridx_ref, rgate_ref):
    x1 = (x_ref[...]
          + jnp.dot(att_ref[...], wo_ref[:ATT_W, :], preferred_element_type=F32)
          + jnp.dot(hm_ref[...], wo_ref[ATT_W:, :], preferred_element_type=F32))
    x1_ref[...] = x1
    r = lax.rsqrt(jnp.mean(x1 * x1, axis=-1, keepdims=True) + EPS)
    h2 = x1 * r * g_ref[...]
    h2_ref[...] = h2
    logits = _dot_f32(h2, wr_ref[...]) + br_ref[...]
    lane = lax.broadcasted_iota(I32, logits.shape, 1)
    ridx = jnp.zeros(logits.shape, I32)
    vals = []
    for kk in range(TOP_K):
        mx = jnp.max(logits, axis=1, keepdims=True)
        am = jnp.min(jnp.where(logits == mx, lane, LANES), axis=1, keepdims=True)
        ridx = jnp.where(lane == kk, am, ridx)
        vals.append(mx)
        logits = jnp.where(lane == am, -jnp.inf, logits)
    es = [jnp.exp(v - vals[0]) for v in vals]
    tot = es[0] + es[1] + es[2] + es[3]
    rgate = jnp.zeros(logits.shape, F32)
    for kk in range(TOP_K):
        rgate = jnp.where(lane == kk, es[kk] / tot, rgate)
    ridx_ref[...] = ridx
    rgate_ref[...] = rgate


def _outproj(x2, att, hm, wo, g_ffn, wr, br, tm):
    n, d = x2.shape
    assert n % tm == 0
    row = lambda w: pl.BlockSpec((tm, w), lambda i: (i, 0))
    cst = lambda s: pl.BlockSpec(s, lambda i: (0, 0))
    return pl.pallas_call(
        _outproj_kernel,
        grid=(n // tm,),
        in_specs=[row(d), row(ATT_W), row(ML_W), cst((ATT_W + ML_W, d)), cst((1, d)), cst((d, LANES)), cst((1, LANES))],
        out_specs=[row(d), row(d), row(LANES), row(LANES)],
        out_shape=[jax.ShapeDtypeStruct((n, d), F32), jax.ShapeDtypeStruct((n, d), F32),
                   jax.ShapeDtypeStruct((n, LANES), I32), jax.ShapeDtypeStruct((n, LANES), F32)],
        compiler_params=_cparams(("parallel",)),
        name="outproj_router",
    )(x2, att, hm, wo, g_ffn.reshape(1, d), wr, br)


def _router_params(w_router, b_router):
    d = w_router.shape[0]
    wr = jnp.zeros((d, LANES), F32).at[:, :N_EXPERTS].set(w_router.astype(F32))
    br = jnp.full((1, LANES), -jnp.inf, F32).at[0, :N_EXPERTS].set(b_router.astype(F32))
    return wr, br


def _route(top_idx, block):
    n_tok = top_idx.shape[0]
    n_asg = n_tok * TOP_K
    assert n_asg % block == 0
    n_blocks = n_asg // block
    e_flat = top_idx.reshape(n_asg).astype(I32)
    order = jnp.argsort(e_flat).astype(I32)
    slot_tok = order // TOP_K
    slot_dst = (order % TOP_K) * n_tok + slot_tok
    counts = jnp.bincount(e_flat, length=N_EXPERTS).astype(I32)
    ends = jnp.cumsum(counts).astype(I32)
    starts = ends - counts
    blk_lo = jnp.arange(n_blocks, dtype=I32) * block
    blk_hi = blk_lo + block
    e_first = jnp.sum(ends[None, :] <= blk_lo[:, None], axis=1).astype(I32)
    e_last = jnp.sum(starts[None, :] < blk_hi[:, None], axis=1).astype(I32) - 1
    n_per = e_last - e_first + 1
    item_start = jnp.cumsum(n_per).astype(I32) - n_per
    n_items = n_blocks + N_EXPERTS - 1
    it = jnp.arange(n_items, dtype=I32)
    blk = jnp.sum(item_start[None, :] <= it[:, None], axis=1).astype(I32) - 1
    k_in = it - item_start[blk]
    real = k_in < n_per[blk]
    exp = jnp.minimum(e_first[blk] + k_in, N_EXPERTS - 1)
    lo = jnp.where(real, jnp.clip(starts[exp], blk_lo[blk], blk_hi[blk]) - blk_lo[blk], 0)
    hi = jnp.where(real, jnp.clip(ends[exp], blk_lo[blk], blk_hi[blk]) - blk_lo[blk], 0)
    exp = jnp.where(real, exp, e_last[n_blocks - 1])
    items = (blk, exp.astype(I32), lo.astype(I32), hi.astype(I32), (k_in == 0).astype(I32))
    return slot_tok.reshape(n_blocks, 1, block), slot_dst.reshape(n_blocks, 1, block), items


def _moe_kernel(blk_ref, exp_ref, lo_ref, hi_ref, first_ref, tok_ref, tokn_ref, dst_ref, x_hbm,
                wg_ref, bg_ref, wu_ref, bu_ref, wd_ref, bd_ref, ys_hbm, xbuf, ybuf, xb_ref, act_ref, gsem, ssem, *, n_blocks):
    i = pl.program_id(0)
    n_items = pl.num_programs(0)
    bs = xbuf.shape[1]
    b = blk_ref[i]
    lo, hi = lo_ref[i], hi_ref[i]
    first = first_ref[i] > 0
    xs = b % 2
    ysl = i % 2

    prefetch = first & (b + 1 < n_blocks)
    busy = hi > lo

    def gather_row(tref, s, r):
        pltpu.make_async_copy(x_hbm.at[pl.ds(tref[0, r], 1)], xbuf.at[s, pl.ds(r, 1)], gsem.at[s]).start()

    def gather_rows(tref, s, r0, r1):
        for r in range(r0, r1):
            gather_row(tref, s, r)

    def gather(tref, s):
        def body(r, _):
            gather_row(tref, s, r)
            return 0
        lax.fori_loop(0, bs, body, 0, unroll=8)

    def wait_scatter(s, n):
        p = bs
        while p >= 1:
            @pl.when((n & p) != 0)
            def _(p=p):
                pltpu.make_async_copy(ybuf.at[s, pl.ds(0, p)], ys_hbm.at[pl.ds(0, p)], ssem.at[s]).wait()
            p //= 2

    @pl.when(i == 0)
    def _first_block():
        gather(tok_ref, 0)

    @pl.when(prefetch & jnp.logical_not(busy))
    def _prefetch_next_block():
        gather(tokn_ref, 1 - xs)

    @pl.when(first)
    def _rows_ready():
        pltpu.make_async_copy(x_hbm.at[pl.ds(0, bs)], xbuf.at[xs], gsem.at[xs]).wait()

    @pl.when(i >= 2)
    def _result_buffer_free():
        wait_scatter(ysl, hi_ref[i - 2] - lo_ref[i - 2])

    def expert(prefetch_rows):
        xb_ref[...] = xbuf[xs].astype(BF16)
        dff = wg_ref.shape[1]
        tiles = dff // MOE_COL_TILE
        per = bs // (2 * tiles)
        seg = 0
        for t in range(tiles):
            cs = slice(t * MOE_COL_TILE, (t + 1) * MOE_COL_TILE)
            g = jnp.minimum(jnp.dot(xb_ref[...], wg_ref[:, cs], preferred_element_type=F32) + bg_ref[:, cs], SWIGLU_LIMIT)
            if prefetch_rows:
                gather_rows(tokn_ref, 1 - xs, seg * per, (seg + 1) * per)
                seg += 1
            u = jnp.clip(jnp.dot(xb_ref[...], wu_ref[:, cs], preferred_element_type=F32) + bu_ref[:, cs],
                         -SWIGLU_LIMIT, SWIGLU_LIMIT)
            if prefetch_rows:
                gather_rows(tokn_ref, 1 - xs, seg * per, (seg + 1) * per if seg + 1 < 2 * tiles else bs)
                seg += 1
            act_ref[:, cs] = ((u + 1.0) * (g * jax.nn.sigmoid(SWIGLU_ALPHA * g))).astype(BF16)
        ybuf[ysl] = jnp.dot(act_ref[...], wd_ref[...], preferred_element_type=F32) + bd_ref[...]

        def scatter(r, _):
            pltpu.make_async_copy(ybuf.at[ysl, pl.ds(r, 1)], ys_hbm.at[pl.ds(dst_ref[0, r], 1)], ssem.at[ysl]).start()
            return 0
        lax.fori_loop(lo, hi, scatter, 0)

    @pl.when(busy & prefetch)
    def _compute_and_prefetch():
        expert(True)

    @pl.when(busy & jnp.logical_not(prefetch))
    def _compute():
        expert(False)

    @pl.when(i == n_items - 1)
    def _drain():
        wait_scatter(ysl, hi - lo)

        @pl.when(i >= 1)
        def _():
            wait_scatter(1 - ysl, hi_ref[i - 1] - lo_ref[i - 1])


def _moe(h2, top_idx, wg, bg, wu, bu, wd, bd, block):
    n, d = h2.shape
    dff = wg.shape[2]
    slot_tok, slot_dst, items = _route(top_idx, block)
    n_blocks = slot_tok.shape[0]
    n_items = items[0].shape[0]
    smem_blk = lambda f: pl.BlockSpec((None, 1, block), f, memory_space=pltpu.SMEM)
    wspec = lambda a, b_: pl.BlockSpec((None, a, b_), lambda i, blk, exp, *_: (exp[i], 0, 0))
    grid_spec = pltpu.PrefetchScalarGridSpec(
        num_scalar_prefetch=5,
        grid=(n_items,),
        in_specs=[smem_blk(lambda i, blk, *_: (blk[i], 0, 0)),
                  smem_blk(lambda i, blk, *_: (jnp.minimum(blk[i] + 1, n_blocks - 1), 0, 0)),
                  smem_blk(lambda i, blk, *_: (blk[i], 0, 0)),
                  pl.BlockSpec(memory_space=pl.ANY),
                  wspec(d, dff), wspec(1, dff), wspec(d, dff), wspec(1, dff), wspec(dff, d), wspec(1, d)],
        out_specs=pl.BlockSpec(memory_space=pl.ANY),
        scratch_shapes=[pltpu.VMEM((2, block, d), F32), pltpu.VMEM((2, block, d), F32),
                        pltpu.VMEM((block, d), BF16), pltpu.VMEM((block, dff), BF16),
                        pltpu.SemaphoreType.DMA((2,)), pltpu.SemaphoreType.DMA((2,))],
    )
    assert dff % MOE_COL_TILE == 0 and block % (2 * (dff // MOE_COL_TILE)) == 0
    return pl.pallas_call(
        functools.partial(_moe_kernel, n_blocks=n_blocks),
        grid_spec=grid_spec,
        out_shape=jax.ShapeDtypeStruct((n * TOP_K, d), F32),
        compiler_params=_cparams(("arbitrary",)),
        name="moe_experts",
    )(*items, slot_tok, slot_tok, slot_dst, h2,
      wg, bg.reshape(N_EXPERTS, 1, dff), wu, bu.reshape(N_EXPERTS, 1, dff), wd, bd.reshape(N_EXPERTS, 1, d))


def _combine_kernel(x1_ref, y0_ref, y1_ref, y2_ref, y3_ref, gate_ref, g_ref, y_ref):
    x = x1_ref[...]
    for kk, ys_ref in enumerate((y0_ref, y1_ref, y2_ref, y3_ref)):
        x = x + gate_ref[:, kk:kk + 1] * ys_ref[...]
    r = lax.rsqrt(jnp.mean(x * x, axis=-1, keepdims=True) + EPS)
    y_ref[...] = x * r * g_ref[...]


def _combine(x1, ys, rgate, g_final, tm):
    n, d = x1.shape
    nt = n // tm
    row = lambda w: pl.BlockSpec((tm, w), lambda i: (i, 0))
    ys_specs = [pl.BlockSpec((tm, d), lambda i, j=j: (j * nt + i, 0)) for j in range(TOP_K)]
    return pl.pallas_call(
        _combine_kernel,
        grid=(nt,),
        in_specs=[row(d)] + ys_specs + [row(LANES), pl.BlockSpec((1, d), lambda i: (0, 0))],
        out_specs=row(d),
        out_shape=jax.ShapeDtypeStruct((n, d), F32),
        compiler_params=_cparams(("parallel",)),
        name="combine_norm",
    )(x1, ys, ys, ys, ys, rgate, g_final.reshape(1, d))


def _page_copies(pt_ref, b, cache_hbm, buf, sem, first_page, n_pages):
    return [pltpu.make_async_copy(cache_hbm.at[pt_ref[b, first_page + p]], buf.at[p], sem) for p in range(n_pages)]


def _dsa_sample_select_kernel(pt_ref, iq_ref, iw_ref, ikown_ref, cache_hbm, bias_ref, kbuf, sem, *, topk, page):
    b = pl.program_id(0)
    nb = pl.num_programs(0)
    past = kbuf.shape[2]
    n_pages = past // page
    nk = bias_ref.shape[1]
    slot = b % 2

    def copies(bb, s):
        return [pltpu.make_async_copy(cache_hbm.at[pt_ref[bb, p]], kbuf.at[s, :, p * page:(p + 1) * page], sem.at[s])
                for p in range(n_pages)]

    @pl.when(b == 0)
    def _first():
        for cp in copies(0, 0):
            cp.start()

    @pl.when(b + 1 < nb)
    def _prefetch():
        for cp in copies(b + 1, 1 - slot):
            cp.start()

    for cp in copies(b, slot):
        cp.wait()

    iq = iq_ref[...]
    w = iw_ref[...]
    s = jnp.dot(iq, kbuf[slot].astype(BF16), preferred_element_type=F32)
    score = jnp.sum(w * jnp.maximum(s, 0.0), axis=0, keepdims=True)
    own = ikown_ref[...].astype(BF16).astype(F32)
    s_own = jnp.sum(iq.astype(F32) * own, axis=1, keepdims=True)
    score_own = jnp.sum(w * jnp.maximum(s_own, 0.0), axis=0, keepdims=True)
    tail_lane = lax.broadcasted_iota(I32, (1, nk - past), 1)
    key = jnp.concatenate([_sort_key(score),
                           jnp.where(tail_lane == 0, _sort_key(score_own), KEY_MASKED)], axis=1)
    idx = lax.broadcasted_iota(I32, (1, nk), 1)

    def count_ge(cand):
        return jnp.sum((key >= cand).astype(I32), axis=1, keepdims=True)

    t_k, _ = _kth_largest_key(count_ge, nk, topk, (1, 1))
    need = topk - count_ge(t_k + 1)
    eq = key == t_k
    nbits = int(nk).bit_length()

    def idx_step(i, jb):
        cand = jb + (jnp.int32(1) << (nbits - 1 - i))
        c = jnp.sum((eq & (idx < cand)).astype(I32), axis=1, keepdims=True)
        return jnp.where(c <= need, cand, jb)

    j_star = lax.fori_loop(0, nbits, idx_step, jnp.zeros((1, 1), I32))
    sel = ((key > t_k) | (eq & (idx < j_star))) & (key > KEY_NEG_INF)
    bias_ref[...] = jnp.where(sel, 0.0, -jnp.inf).astype(F32)


def _dsa_sample_attend_kernel(pt_ref, q_ref, kown_ref, vown_ref, bias_ref, bias_own_ref, ck_hbm, cv_hbm, o_ref,
                              kbuf, vbuf, ksem, vsem, acc_ref, m_ref, l_ref):
    b = pl.program_id(0)
    c = pl.program_id(1)
    nb, ncg = pl.num_programs(0), pl.num_programs(1)
    pg, page = kbuf.shape[1], kbuf.shape[4]
    step = b * ncg + c
    slot = step % 2

    def copies(bb, cc, s):
        return (_page_copies(pt_ref, bb, ck_hbm, kbuf.at[s], ksem.at[s], cc * pg, pg)
                + _page_copies(pt_ref, bb, cv_hbm, vbuf.at[s], vsem.at[s], cc * pg, pg))

    @pl.when(step == 0)
    def _first():
        for cp in copies(0, 0, 0):
            cp.start()

    @pl.when(step + 1 < nb * ncg)
    def _prefetch():
        nxt = step + 1
        for cp in copies(nxt // ncg, nxt % ncg, 1 - slot):
            cp.start()

    @pl.when(c == 0)
    def _init():
        acc_ref[...] = jnp.zeros_like(acc_ref)
        m_ref[...] = jnp.full(m_ref.shape, -1e30, F32)
        l_ref[...] = jnp.zeros_like(l_ref)

    for cp in copies(b, c, slot):
        cp.wait()

    for h in range(ATT_HEADS):
        qc = q_ref[h]
        s = [jnp.sum(kbuf[slot, g, h] * qc, axis=0, keepdims=True) + bias_ref[:, g * page:(g + 1) * page]
             for g in range(pg)]
        top = functools.reduce(jnp.maximum, s)
        m_old = m_ref[h]
        m_new = jnp.maximum(m_old, jnp.max(top, axis=1, keepdims=True))
        alpha = jnp.exp2(m_old - m_new)
        acc = alpha * acc_ref[h]
        lsum = alpha * l_ref[h]
        for g in range(pg):
            p = jnp.exp2(s[g] - m_new)
            lsum = lsum + p
            acc = acc + p * vbuf[slot, g, h]
        acc_ref[h] = acc
        l_ref[h] = lsum
        m_ref[h] = m_new

    @pl.when(c == ncg - 1)
    def _finish():
        for h in range(ATT_HEADS):
            s_own = jnp.sum(q_ref[h] * kown_ref[h], axis=0, keepdims=True) + bias_own_ref[:, 0:1]
            m_old = m_ref[h]
            m_new = jnp.maximum(m_old, s_own)
            alpha = jnp.exp2(m_old - m_new)
            p_own = jnp.exp2(s_own - m_new)[:, 0:1]
            l_tot = jnp.sum(alpha * l_ref[h], axis=1, keepdims=True) + p_own
            o = jnp.sum(alpha * acc_ref[h], axis=1, keepdims=True) + p_own * vown_ref[h]
            o_ref[h] = o / l_tot


def _dsa_sample(q, k_own, v_own, iq, ik_own, iw, cache_k, cache_v, cache_ik, page_table, pages_per_step):
    bsz = q.shape[0]
    n_pool, page = cache_ik.shape[:2]
    n_pages = page_table.shape[1]
    past = n_pages * page
    topk = min(TOPK_MAX, (past + 1) // 4)
    nk = past + LANES
    assert n_pages % pages_per_step == 0
    blk3 = lambda s: pl.BlockSpec((None,) + s, lambda b, *_: (b, 0, 0))
    blk4 = lambda s: pl.BlockSpec((None,) + s, lambda b, *_: (b, 0, 0, 0))
    bias = pl.pallas_call(
        functools.partial(_dsa_sample_select_kernel, topk=topk, page=page),
        grid_spec=pltpu.PrefetchScalarGridSpec(
            num_scalar_prefetch=1, grid=(bsz,),
            in_specs=[blk3((IDX_HEADS, IDX_DIM)), blk3((IDX_HEADS, 1)), blk3((1, IDX_DIM)),
                      pl.BlockSpec(memory_space=pl.ANY)],
            out_specs=blk3((1, nk)),
            scratch_shapes=[pltpu.VMEM((2, IDX_DIM, past), F32), pltpu.SemaphoreType.DMA((2,))]),
        out_shape=jax.ShapeDtypeStruct((bsz, 1, nk), F32),
        compiler_params=_cparams(("arbitrary",)),
        name="dsa_sample_select",
    )(page_table, iq.reshape(bsz, IDX_HEADS, IDX_DIM), iw.reshape(bsz, IDX_HEADS, 1), ik_own.reshape(bsz, 1, IDX_DIM),
      jnp.transpose(cache_ik, (0, 2, 1)))
    ncg = n_pages // pages_per_step
    keys_per_step = pages_per_step * page
    bias_past = bias[:, :, :past].reshape(bsz, ncg, 1, keys_per_step)
    bias_own = bias[:, :, past:]
    col = lambda a: a.astype(F32).reshape(bsz, ATT_HEADS, ATT_DIM, 1)
    pages = pltpu.VMEM((2, pages_per_step, ATT_HEADS, ATT_DIM, page), F32)
    att = pl.pallas_call(
        _dsa_sample_attend_kernel,
        grid_spec=pltpu.PrefetchScalarGridSpec(
            num_scalar_prefetch=1, grid=(bsz, ncg),
            in_specs=[blk4((ATT_HEADS, ATT_DIM, 1)), blk4((ATT_HEADS, ATT_DIM, 1)), blk4((ATT_HEADS, ATT_DIM, 1)),
                      pl.BlockSpec((None, None, 1, keys_per_step), lambda b, c, *_: (b, c, 0, 0)), blk3((1, LANES)),
                      pl.BlockSpec(memory_space=pl.ANY), pl.BlockSpec(memory_space=pl.ANY)],
            out_specs=blk4((ATT_HEADS, ATT_DIM, 1)),
            scratch_shapes=[pages, pages, pltpu.SemaphoreType.DMA((2,)), pltpu.SemaphoreType.DMA((2,)),
                            pltpu.VMEM((ATT_HEADS, ATT_DIM, page), F32), pltpu.VMEM((ATT_HEADS, 1, page), F32),
                            pltpu.VMEM((ATT_HEADS, 1, page), F32)]),
        out_shape=jax.ShapeDtypeStruct((bsz, ATT_HEADS, ATT_DIM, 1), F32),
        compiler_params=_cparams(("arbitrary", "arbitrary")),
        name="dsa_sample_attend",
    )(page_table, col(q), col(k_own), col(v_own), bias_past, bias_own,
      jnp.transpose(cache_k, (0, 2, 3, 1)), jnp.transpose(cache_v, (0, 2, 3, 1)))
    return att.reshape(bsz, ATT_W).astype(BF16)


def _mlstm_step_kernel(q_ref, k_ref, v_ref, mo_ref, g_ref, gate_ref, gbias_ref, c_ref, n_ref, m_ref,
                       hm_ref, c_out_ref, n_out_ref, m_out_ref):
    gates = _softcap(gate_ref[...] + gbias_ref[...])
    lane = lax.broadcasted_iota(I32, (1, LANES), 1)
    m_out = jnp.zeros((1, LANES), F32)
    for h in range(ML_HEADS):
        ig = gates[:, M_MI + h:M_MI + h + 1]
        lf = _log_sigmoid(gates[:, M_MF + h:M_MF + h + 1])
        m_prev = m_ref[:, h:h + 1]
        q = q_ref[:, h * ML_QK:(h + 1) * ML_QK].astype(F32)
        k = k_ref[:, h * ML_QK:(h + 1) * ML_QK].astype(F32)
        v = v_ref[h].astype(F32)
        c = c_ref[h]
        n = n_ref[h:h + 1, :]
        m_inter = lf + m_prev
        m_t = jnp.maximum(m_inter, ig)
        w_intra = jnp.exp(ig - m_t)
        w_inter = jnp.exp(m_inter - m_t)
        qk = jnp.sum(q * k, axis=1, keepdims=True) * w_intra
        num = w_inter * jnp.sum(c * q, axis=1, keepdims=True) + qk * v
        den = w_inter * jnp.sum(n * q, axis=1, keepdims=True) + qk
        hh = num / jnp.maximum(jnp.abs(den), jnp.exp(-m_t))
        hh = hh * lax.rsqrt(jnp.mean(hh * hh, axis=0, keepdims=True) + EPS)
        hm_ref[h] = (hh * g_ref[h] * jax.nn.sigmoid(mo_ref[h])).astype(hm_ref.dtype)
        decay = jnp.exp(lf + m_prev - m_t)
        w_s = jnp.exp(ig - m_t)
        c_out_ref[h] = decay * c + (w_s * v) * k
        n_out_ref[h:h + 1, :] = decay * n + w_s * k
        m_out = jnp.where(lane == h, m_t, m_out)
    m_out_ref[...] = m_out


def _mlstm_step(mq, mk, mv, mo, misc, b_igate, b_fgate, g_mlstm, state_c, state_n, state_m):
    bsz = mq.shape[0]
    _, brow = _gate_bias(b_igate, b_fgate)
    b3 = lambda s: pl.BlockSpec((None,) + s, lambda b: (b,) + (0,) * len(s))
    cst = lambda s: pl.BlockSpec(s, lambda b: (0,) * len(s))
    hm, c_new, n_new, m_new = pl.pallas_call(
        _mlstm_step_kernel,
        grid=(bsz,),
        in_specs=[b3((1, ML_QW)), b3((1, ML_QW)), b3((ML_HEADS, ML_V, 1)), b3((ML_HEADS, ML_V, 1)), cst((ML_HEADS, ML_V, 1)),
                  b3((1, LANES)), cst((1, LANES)), b3((ML_HEADS, ML_V, ML_QK)), b3((ML_HEADS, ML_QK)), b3((1, ML_HEADS))],
        out_specs=[b3((ML_HEADS, ML_V, 1)), b3((ML_HEADS, ML_V, ML_QK)), b3((ML_HEADS, ML_QK)), b3((1, LANES))],
        out_shape=[jax.ShapeDtypeStruct((bsz, ML_HEADS, ML_V, 1), F32),
                   jax.ShapeDtypeStruct((bsz, ML_HEADS, ML_V, ML_QK), F32),
                   jax.ShapeDtypeStruct((bsz, ML_HEADS, ML_QK), F32),
                   jax.ShapeDtypeStruct((bsz, 1, LANES), F32)],
        compiler_params=_cparams(("parallel",)),
        name="mlstm_step",
    )(mq.reshape(bsz, 1, ML_QW), mk.reshape(bsz, 1, ML_QW), mv.astype(F32).reshape(bsz, ML_HEADS, ML_V, 1),
      mo.reshape(bsz, ML_HEADS, ML_V, 1), g_mlstm.astype(F32).reshape(ML_HEADS, ML_V, 1),
      misc.reshape(bsz, 1, LANES), brow, state_c, state_n, state_m.reshape(bsz, 1, ML_HEADS))
    return hm.reshape(bsz, ML_W).astype(BF16), c_new, n_new, m_new[:, 0, :ML_HEADS]


TQ = 512
TM_PROJ = 512
ML_CHUNK = 256
MOE_BLOCK_PROMPT = 512
MOE_BLOCK_SAMPLE = 128
MOE_COL_TILE = 256
SAMPLE_PAGES_PER_STEP = 8


def kernel(x_prompt, x_sample, cache_k, cache_v, cache_idx_k, state_C, state_n, state_m, page_table, g_mix, w_in, b_igate, b_fgate, g_mlstm, w_out, g_ffn, w_router, b_router, w_gate, b_gate, w_up, b_up, w_down, b_down, g_final):
    bp, tp, d = x_prompt.shape
    bs, ts, _ = x_sample.shape
    assert w_in.shape[0] == 1 and ts == 1, "one layer, one new token per sampled sequence"
    page = cache_k.shape[2]
    n_pages = page_table.shape[1]
    past = n_pages * page

    w = _build_w_in(w_in[0])
    wo = w_out[0].astype(BF16)
    wr, br = _router_params(w_router[0], b_router[0])
    moe_w = (w_gate[0].astype(BF16), b_gate[0], w_up[0].astype(BF16), b_up[0], w_down[0].astype(BF16), b_down[0])

    def tail(x2, att, hm, tm, moe_block):
        x1, h2, ridx, rgate = _outproj(x2, att, hm, wo, g_ffn[0], wr, br, tm)
        ys = _moe(h2, ridx[:, :TOP_K], *moe_w, moe_block)
        return _combine(x1, ys, rgate, g_final, tm)

    xp2 = x_prompt.reshape(bp * tp, d)
    cos_p, sin_p = _rope_tables(jnp.arange(tp))
    q, kf, kb, vf, vb, iq, misc, mq, mk, mv, mo = _inproj(xp2, g_mix[0], w, cos_p, sin_p, TM_PROJ)
    ik_p = misc[:, M_IK:M_IK + IDX_DIM]
    att = _dsa_prompt(q, kb, vb, iq, ik_p.astype(BF16), misc[:, M_IW:M_IW + IDX_HEADS], bp, tp, TQ)
    hm, c_p, n_p, m_p = _mlstm_prompt(mq, mk, mv, mo, misc, b_igate[0], b_fgate[0], g_mlstm[0], bp, tp, ML_CHUNK)
    y_prompt = tail(xp2, att, hm, TM_PROJ, MOE_BLOCK_PROMPT).reshape(bp, tp, d)

    xs2 = x_sample.reshape(bs, d)
    cos_s, sin_s = _rope_tables(jnp.full((bs,), past))
    q_s, kf_s, kb_s, vf_s, vb_s, iq_s, misc_s, mq_s, mk_s, mv_s, mo_s = _inproj(xs2, g_mix[0], w, cos_s, sin_s, bs)
    ik_s = misc_s[:, M_IK:M_IK + IDX_DIM]
    att_s = _dsa_sample(q_s, kf_s, vf_s, iq_s, ik_s, misc_s[:, M_IW:M_IW + IDX_HEADS],
                        cache_k[0], cache_v[0], cache_idx_k[0], page_table, math.gcd(SAMPLE_PAGES_PER_STEP, n_pages))
    hm_s, c_s, n_s, m_s = _mlstm_step(mq_s, mk_s, mv_s, mo_s, misc_s, b_igate[0], b_fgate[0], g_mlstm[0],
                                      state_C[0], state_n[0], state_m[0])
    y_sample = tail(xs2, att_s, hm_s, bs, min(MOE_BLOCK_SAMPLE, bs * TOP_K)).reshape(bs, ts, d)

    return (y_prompt, y_sample,
            kf.reshape(1, bp, tp // page, page, ATT_HEADS, ATT_DIM), vf.reshape(1, bp, tp // page, page, ATT_HEADS, ATT_DIM),
            ik_p.reshape(1, bp, tp // page, page, IDX_DIM), c_p[None], n_p[None], m_p[None],
            kf_s.reshape(1, bs, ts, ATT_HEADS, ATT_DIM), vf_s.reshape(1, bs, ts, ATT_HEADS, ATT_DIM),
            ik_s.reshape(1, bs, ts, IDX_DIM), c_s[None], n_s[None], m_s[None])
```

```python
import functools
import math

import jax
import jax.numpy as jnp
import numpy as np
from jax import lax
from jax.experimental import pallas as pl
from jax.experimental.pallas import tpu as pltpu

F32 = jnp.float32
BF16 = jnp.bfloat16
I32 = jnp.int32
I16 = jnp.int16

ATT_HEADS = 8
ATT_DIM = 64
IDX_HEADS = 8
IDX_DIM = 64
TOPK_MAX = 256
ML_HEADS = 4
ML_QK = 64
ML_V = 128
GATE_CAP = 15.0
N_EXPERTS = 32
TOP_K = 4
SWIGLU_LIMIT = 7.0
SWIGLU_ALPHA = 1.702
ROPE_THETA = 10000.0
EPS = 1e-5

ATT_W = ATT_HEADS * ATT_DIM
IDX_W = IDX_HEADS * IDX_DIM
ML_QW = ML_HEADS * ML_QK
ML_W = ML_HEADS * ML_V

LANES = 128
Q_SCALE = (ATT_DIM ** -0.5) * math.log2(math.e)
VMEM_LIMIT = 56 * 1024 * 1024

KEY_MASKED = -(2 ** 31)
KEY_NEG_INF = int(np.int32(np.uint32(0xFF800000)) ^ np.int32(0x7FFFFFFF))

C_AQ, C_AK, C_AV, C_IQ, C_MQ, C_MK, C_MV, C_MO, C_MISC = 0, 512, 1024, 1536, 2048, 2304, 2560, 3072, 3584
C_AQR, C_AKR, C_IQR, C_MISCR, C_END = 3712, 4224, 4736, 5248, 5376
M_IK, M_IW, M_MI, M_MF = 0, 64, 72, 76


def _cparams(sem, vmem=VMEM_LIMIT):
    return pltpu.CompilerParams(dimension_semantics=sem, vmem_limit_bytes=vmem)


def _rot_cols(w):
    d, n = w.shape
    w4 = w.reshape(d, n // 64, 2, 32)
    return jnp.stack([w4[:, :, 1, :], w4[:, :, 0, :]], axis=2).reshape(d, n)


def _build_w_in(w_in):
    d = w_in.shape[0]
    sp = np.cumsum([0, ATT_W, ATT_W, ATT_W, IDX_W, IDX_HEADS, IDX_DIM, ML_QW, ML_QW, ML_W, ML_W, ML_HEADS, ML_HEADS])
    aq, ak, av, iq, iw, ik, mq, mk, mv, mo, mi, mf = [w_in[:, sp[i]:sp[i + 1]] for i in range(12)]
    misc = jnp.concatenate([ik, iw, mi, mf, jnp.zeros((d, LANES - 80), w_in.dtype)], axis=1)
    miscr = jnp.concatenate([_rot_cols(ik), jnp.zeros((d, LANES - 64), w_in.dtype)], axis=1)
    w = jnp.concatenate([aq, ak, av, iq, mq, mk, mv, mo, misc, _rot_cols(aq), _rot_cols(ak), _rot_cols(iq), miscr], axis=1)
    assert w.shape[1] == C_END
    return w.astype(BF16)


def _rope_tables(pos):
    half = ATT_DIM // 2
    inv = ROPE_THETA ** (-jnp.arange(half, dtype=F32) / half)
    ang = pos.astype(F32)[:, None] * inv[None, :]
    cos, sin = jnp.cos(ang), jnp.sin(ang)
    cos_t = jnp.concatenate([cos, cos, cos, cos], axis=1)
    sin_t = jnp.concatenate([-sin, sin, -sin, sin], axis=1)
    return cos_t, sin_t


def _inproj_kernel(x_ref, g_ref, w_ref, cos_ref, sin_ref,
                   q_ref, kf_ref, kb_ref, vf_ref, vb_ref, iq_ref, misc_ref, mq_ref, mk_ref, mv_ref, mo_ref):
    x = x_ref[...]
    r = lax.rsqrt(jnp.mean(x * x, axis=-1, keepdims=True) + EPS)
    h = (x * r * g_ref[...]).astype(BF16)
    cos = cos_ref[...]
    sin = sin_ref[...]

    def dot(c0, n):
        return jnp.dot(h, w_ref[:, c0:c0 + n], preferred_element_type=F32)

    def rope(c0, cr, n, lane_cos, lane_sin):
        z = dot(c0, n)
        zr = dot(cr, n)
        return [z[:, j:j + LANES] * lane_cos + zr[:, j:j + LANES] * lane_sin for j in range(0, n, LANES)]

    for j, t in enumerate(rope(C_AQ, C_AQR, ATT_W, cos, sin)):
        q_ref[:, j * LANES:(j + 1) * LANES] = (t * Q_SCALE).astype(BF16)
    for j, t in enumerate(rope(C_AK, C_AKR, ATT_W, cos, sin)):
        kf_ref[:, j * LANES:(j + 1) * LANES] = t
        kb_ref[:, j * LANES:(j + 1) * LANES] = t.astype(BF16)
    v = dot(C_AV, ATT_W)
    vf_ref[...] = v
    vb_ref[...] = v.astype(BF16)
    for j, t in enumerate(rope(C_IQ, C_IQR, IDX_W, cos, sin)):
        iq_ref[:, j * LANES:(j + 1) * LANES] = t.astype(BF16)
    lane = lax.broadcasted_iota(I32, cos.shape, 1)
    mcos = jnp.where(lane < IDX_DIM, cos, 1.0)
    msin = jnp.where(lane < IDX_DIM, sin, 0.0)
    misc_ref[...] = rope(C_MISC, C_MISCR, LANES, mcos, msin)[0]
    mq_ref[...] = dot(C_MQ, ML_QW).astype(BF16)
    mk_ref[...] = (dot(C_MK, ML_QW) * (ML_QK ** -0.5)).astype(BF16)
    mv_ref[...] = dot(C_MV, ML_W).astype(BF16)
    mo_ref[...] = dot(C_MO, ML_W)


def _inproj(x2, g, w, cos_t, sin_t, tm):
    n, d = x2.shape
    tt = cos_t.shape[0]
    assert n % tm == 0 and tt % tm == 0
    nt = tt // tm
    row = lambda i: (i, 0)
    outs = [(ATT_W, BF16), (ATT_W, F32), (ATT_W, BF16), (ATT_W, F32), (ATT_W, BF16), (IDX_W, BF16), (LANES, F32),
            (ML_QW, BF16), (ML_QW, BF16), (ML_W, BF16), (ML_W, F32)]
    return pl.pallas_call(
        _inproj_kernel,
        grid=(n // tm,),
        in_specs=[pl.BlockSpec((tm, d), row),
                  pl.BlockSpec((1, d), lambda i: (0, 0)),
                  pl.BlockSpec((d, C_END), lambda i: (0, 0), pipeline_mode=pl.Buffered(1)),
                  pl.BlockSpec((tm, LANES), lambda i: (i % nt, 0)),
                  pl.BlockSpec((tm, LANES), lambda i: (i % nt, 0))],
        out_specs=[pl.BlockSpec((tm, c), row) for c, _ in outs],
        out_shape=[jax.ShapeDtypeStruct((n, c), dt) for c, dt in outs],
        compiler_params=_cparams(("parallel",)),
        name="inproj",
    )(x2, g.reshape(1, d), w, cos_t, sin_t)


def _sort_key(score):
    bits = lax.bitcast_convert_type(score, I32)
    return bits ^ ((bits >> 31) & 0x7FFFFFFF)


BITS_PER_CHECK = 4


def _kth_largest_key(count_ge, total, k, shape, count_ge_hi=None):
    def pending(cnt):
        return jnp.max((cnt != k).astype(I32))

    t0 = jnp.full(shape, KEY_MASKED, I32)
    cnt0 = jnp.full(shape, total, I32)
    i0 = jnp.int32(0)
    if count_ge_hi is not None:
        def hi_step(i, carry):
            t, cnt = carry
            cand = t + (jnp.int32(1) << (15 - i))
            c = count_ge_hi(cand)
            ok = c >= k
            return jnp.where(ok, cand, t), jnp.where(ok, c, cnt)

        t_hi, cnt0 = lax.fori_loop(0, 16, hi_step, (jnp.full(shape, -(1 << 15), I32), cnt0))
        t0 = t_hi << 16
        i0 = jnp.int32(16)

    def bit_steps(carry):
        i, _, t, cnt = carry
        for _ in range(BITS_PER_CHECK):
            cand = t + (jnp.int32(1) << (31 - i))
            c = count_ge(cand)
            ok = c >= k
            t = jnp.where(ok, cand, t)
            cnt = jnp.where(ok, c, cnt)
            i = i + 1
        return i, pending(cnt), t, cnt

    _, _, t, cnt = lax.while_loop(lambda c: (c[0] < 32) & (c[1] > 0), bit_steps, (i0, pending(cnt0), t0, cnt0))
    return t, cnt


def _dsa_prompt_kernel(qT_ref, iqT_ref, iwT_ref, kc_ref, vT_ref, ikc_ref, oT_ref,
                       keys_ref, keys_hi_ref, qpad_ref, bias_ref, *head_refs, topk):
    j = pl.program_id(1)
    nck = j + 1
    tq = keys_ref.shape[1]
    kc_sz = kc_ref.shape[1]
    assert kc_sz == tq

    def chunk_rows(c):
        return pl.ds(pl.multiple_of(c * kc_sz, kc_sz), kc_sz)

    row = lax.broadcasted_iota(I32, (kc_sz, tq), 0)
    col = lax.broadcasted_iota(I32, (kc_sz, tq), 1)

    def score_chunk(c, _):
        ik_c = ikc_ref[c]
        acc = jnp.zeros((kc_sz, tq), F32)
        for h in range(IDX_HEADS):
            s = jnp.dot(ik_c, iqT_ref[h * IDX_DIM:(h + 1) * IDX_DIM, :], preferred_element_type=F32)
            acc = acc + iwT_ref[h:h + 1, :] * jnp.maximum(s, 0.0)
        admissible = (c < j) | (row <= col)
        key = jnp.where(admissible, _sort_key(acc), KEY_MASKED)
        keys_ref[chunk_rows(c), :] = key
        keys_hi_ref[chunk_rows(c), :] = (key >> 16).astype(I16)
        return 0

    lax.fori_loop(0, nck, score_chunk, 0)

    def count_ge(cand):
        def body(c, acc):
            ge = (keys_ref[chunk_rows(c), :] >= cand).astype(I32)
            return acc + ge.reshape(kc_sz // 8, 8, tq).sum(axis=0)
        acc = lax.fori_loop(0, nck, body, jnp.zeros((8, tq), I32))
        return acc.sum(axis=0, keepdims=True)

    def count_ge_hi(cand):
        c16 = cand.astype(I16)

        def body(c, acc):
            ge = (keys_hi_ref[chunk_rows(c), :] >= c16).astype(I16)
            for r in range(0, kc_sz, 16):
                acc = acc + ge[r:r + 16, :]
            return acc
        acc = lax.fori_loop(0, nck, body, jnp.zeros((16, tq), I16))
        return acc.astype(I32).sum(axis=0, keepdims=True)

    t_k, cnt = _kth_largest_key(count_ge, nck * kc_sz, topk, (1, tq), count_ge_hi)
    ties = (cnt > topk) & (t_k > KEY_NEG_INF)

    @pl.when(jnp.max(ties.astype(I32)) > 0)
    def _drop_late_ties():
        need = (topk - count_ge(t_k + 1)).astype(F32)
        ltri = (col < row).astype(BF16)

        def body(c, carry):
            kc = keys_ref[chunk_rows(c), :]
            eq = (kc == t_k) & ties
            eqf = jnp.where(eq, 1.0, 0.0).astype(BF16)
            rank = jnp.dot(ltri, eqf, preferred_element_type=F32) + carry
            keys_ref[chunk_rows(c), :] = jnp.where(eq & (rank >= need), KEY_MASKED, kc)
            return carry + jnp.sum(eqf.astype(F32), axis=0, keepdims=True)

        lax.fori_loop(0, nck, body, jnp.zeros((1, tq), F32))

    t_sel = jnp.maximum(t_k, KEY_NEG_INF + 1)

    hrow = lax.broadcasted_iota(I32, (LANES, tq), 0)
    for h in range(ATT_HEADS):
        pair = qT_ref[(h // 2) * LANES:(h // 2 + 1) * LANES, :].astype(F32)
        mine = (hrow >= (h % 2) * ATT_DIM) & (hrow < (h % 2 + 1) * ATT_DIM)
        qpad_ref[h] = jnp.where(mine, pair, 0.0).astype(BF16)
    accs, ms, ls = head_refs[:ATT_HEADS], head_refs[ATT_HEADS:2 * ATT_HEADS], head_refs[2 * ATT_HEADS:]
    for h in range(ATT_HEADS):
        accs[h][...] = jnp.zeros_like(accs[h])
        ms[h][...] = jnp.full(ms[h].shape, -1e30, F32)
        ls[h][...] = jnp.zeros_like(ls[h])

    def attend_chunk(c, _):
        bias_ref[...] = jnp.where(keys_ref[chunk_rows(c), :] >= t_sel, 0.0, -jnp.inf).astype(F32)

        def qk(h):
            kp = kc_ref[c, :, (h // 2) * LANES:(h // 2 + 1) * LANES]
            return jnp.dot(kp, qpad_ref[h], preferred_element_type=F32)

        s_next = qk(0)
        for h in range(ATT_HEADS):
            s = s_next + bias_ref[...]
            if h + 1 < ATT_HEADS:
                s_next = qk(h + 1)
            m_old = ms[h][...]
            m_new = jnp.maximum(m_old, jnp.max(s, axis=0, keepdims=True))
            alpha = jnp.exp2(m_old - m_new)
            p = jnp.exp2(s - m_new)
            ls[h][...] = alpha * ls[h][...] + jnp.sum(p, axis=0, keepdims=True)
            ms[h][...] = m_new
            pv = jnp.dot(vT_ref[c, h * ATT_DIM:(h + 1) * ATT_DIM, :], p.astype(BF16), preferred_element_type=F32)
            accs[h][...] = alpha * accs[h][...] + pv
        return 0

    lax.fori_loop(0, nck, attend_chunk, 0)
    for h in range(ATT_HEADS):
        oT_ref[h * ATT_DIM:(h + 1) * ATT_DIM, :] = (accs[h][...] / ls[h][...]).astype(oT_ref.dtype)


def _dsa_prompt(q, k, v, iq, ik, iw, bsz, t_len, tq):
    assert t_len % tq == 0
    nq = t_len // tq
    topk = min(TOPK_MAX, t_len // 4)

    def to_t(a):
        return a.reshape(bsz, nq, tq, a.shape[-1]).swapaxes(2, 3)

    qT, iqT, iwT, vT = to_t(q), to_t(iq), to_t(iw), to_t(v)
    kc = k.reshape(bsz, nq, tq, ATT_W)
    ikc = ik.reshape(bsz, nq, tq, IDX_DIM)
    per_q = lambda c: pl.BlockSpec((None, None, c, tq), lambda b, j: (b, j, 0, 0))
    per_b = lambda s: pl.BlockSpec((None,) + s, lambda b, j: (b, 0, 0, 0), pipeline_mode=pl.Buffered(1))
    oT = pl.pallas_call(
        functools.partial(_dsa_prompt_kernel, topk=topk),
        grid=(bsz, nq),
        in_specs=[per_q(ATT_W), per_q(IDX_W), per_q(IDX_HEADS),
                  per_b((nq, tq, ATT_W)), per_b((nq, ATT_W, tq)), per_b((nq, tq, IDX_DIM))],
        out_specs=per_q(ATT_W),
        out_shape=jax.ShapeDtypeStruct((bsz, nq, ATT_W, tq), BF16),
        scratch_shapes=([pltpu.VMEM((t_len, tq), I32), pltpu.VMEM((t_len, tq), I16),
                         pltpu.VMEM((ATT_HEADS, LANES, tq), BF16), pltpu.VMEM((tq, tq), F32)]
                        + [pltpu.VMEM((ATT_DIM, tq), F32)] * ATT_HEADS + [pltpu.VMEM((1, tq), F32)] * (2 * ATT_HEADS)),
        compiler_params=_cparams(("parallel", "arbitrary")),
        name="dsa_prompt",
    )(qT, iqT, iwT, kc, vT, ikc)
    return oT.swapaxes(2, 3).reshape(bsz * t_len, ATT_W)


def _softcap(a):
    return GATE_CAP * jnp.tanh(a / GATE_CAP)


def _log_sigmoid(a):
    return -(jnp.maximum(-a, 0.0) + jnp.log1p(jnp.exp(-jnp.abs(a))))


def _dot_f32(a, b):
    return jnp.dot(a, b, precision=lax.Precision.HIGHEST, preferred_element_type=F32)


def _mlstm_kernel(q_ref, kT_ref, v_ref, mo_ref, gcol_ref, grow_ref, bcol_ref, brow_ref, g_ref,
                  hm_ref, s_out_ref, m_out_ref, s_ref, m_ref):
    c = pl.program_id(1)
    L = q_ref.shape[0]

    @pl.when(c == 0)
    def _init():
        s_ref[...] = jnp.zeros_like(s_ref)
        m_ref[...] = jnp.zeros_like(m_ref)

    row = lax.broadcasted_iota(I32, (L, L), 0)
    col = lax.broadcasted_iota(I32, (L, L), 1)
    causal = col <= row
    tri_l = causal.astype(F32)
    tri_u = (row <= col).astype(F32)

    gc = _softcap(gcol_ref[...] + brow_ref[...])
    b_col = _dot_f32(tri_l, _log_sigmoid(gc))
    gr = _softcap(grow_ref[...] + bcol_ref[...])
    b_row = _dot_f32(_log_sigmoid(gr), tri_u)
    ones_col = (lax.broadcasted_iota(I32, (L, ML_V), 1) == 0).astype(BF16)

    for h in range(ML_HEADS):
        bc = b_col[:, M_MF + h:M_MF + h + 1]
        br = b_row[ML_HEADS + h:ML_HEADS + h + 1, :]
        ir = gr[h:h + 1, :]
        m_prev = m_ref[h:h + 1, 0:1]
        dmat = jnp.where(causal, bc - br + ir, -jnp.inf)
        m_inter = bc + m_prev
        m_t = jnp.maximum(m_inter, jnp.max(dmat, axis=1, keepdims=True))
        w_intra = jnp.exp(dmat - m_t)
        w_inter = jnp.exp(m_inter - m_t)
        qh = q_ref[:, h * ML_QK:(h + 1) * ML_QK]
        kth = kT_ref[h * ML_QK:(h + 1) * ML_QK, :]
        vext = jnp.concatenate([v_ref[:, h * ML_V:(h + 1) * ML_V], ones_col], axis=1)
        s_h = s_ref[h]
        qk = jnp.dot(qh, kth, preferred_element_type=F32) * w_intra
        hext = (w_inter * jnp.dot(qh, s_h.astype(BF16), preferred_element_type=F32)
                + jnp.dot(qk.astype(BF16), vext, preferred_element_type=F32))
        num = hext[:, :ML_V]
        den = hext[:, ML_V:ML_V + 1]
        hh = num / jnp.maximum(jnp.abs(den), jnp.exp(-m_t))
        hh = hh * lax.rsqrt(jnp.mean(hh * hh, axis=-1, keepdims=True) + EPS)
        hs = slice(h * ML_V, (h + 1) * ML_V)
        hm_ref[:, hs] = (hh * g_ref[:, hs] * jax.nn.sigmoid(mo_ref[:, hs])).astype(hm_ref.dtype)
        b_last = bc[L - 1:L, :]
        m_new = m_t[L - 1:L, :]
        decay = jnp.exp(b_last + m_prev - m_new)
        w_s = jnp.exp(b_last - br + ir - m_new)
        kw = (kth.astype(F32) * w_s).astype(BF16)
        s_ref[h] = decay * s_h + jnp.dot(kw, vext, preferred_element_type=F32)
        m_ref[h:h + 1, :] = jnp.broadcast_to(m_new, (1, LANES))

    @pl.when(c == pl.num_programs(1) - 1)
    def _emit_state():
        s_out_ref[...] = s_ref[...]
        m_out_ref[...] = m_ref[...]


def _gate_bias(b_igate, b_fgate):
    bias8 = jnp.concatenate([b_igate, b_fgate]).astype(F32)
    brow = jnp.zeros((1, LANES), F32).at[0, M_MI:M_MI + 2 * ML_HEADS].set(bias8)
    return bias8.reshape(2 * ML_HEADS, 1), brow


def _mlstm_prompt(mq, mk, mv, mo, misc, b_igate, b_fgate, g_mlstm, bsz, t_len, L):
    assert t_len % L == 0
    nc = t_len // L
    kT = mk.reshape(bsz, nc, L, ML_QW).swapaxes(2, 3)
    grow = misc[:, M_MI:M_MI + 2 * ML_HEADS].reshape(bsz, nc, L, 2 * ML_HEADS).swapaxes(2, 3)
    bcol, brow = _gate_bias(b_igate, b_fgate)
    rows = lambda w: pl.BlockSpec((L, w), lambda b, c: (b * nc + c, 0))
    cst = lambda s: pl.BlockSpec(s, lambda b, c: (0,) * len(s))
    hm, s_out, m_out = pl.pallas_call(
        _mlstm_kernel,
        grid=(bsz, nc),
        in_specs=[rows(ML_QW), pl.BlockSpec((None, None, ML_QW, L), lambda b, c: (b, c, 0, 0)), rows(ML_W), rows(ML_W),
                  rows(LANES), pl.BlockSpec((None, None, 2 * ML_HEADS, L), lambda b, c: (b, c, 0, 0)),
                  cst((2 * ML_HEADS, 1)), cst((1, LANES)), cst((1, ML_W))],
        out_specs=[rows(ML_W), pl.BlockSpec((None, ML_HEADS, ML_QK, 2 * ML_V), lambda b, c: (b, 0, 0, 0)),
                   pl.BlockSpec((None, 2 * ML_HEADS, LANES), lambda b, c: (b, 0, 0))],
        out_shape=[jax.ShapeDtypeStruct((bsz * t_len, ML_W), BF16),
                   jax.ShapeDtypeStruct((bsz, ML_HEADS, ML_QK, 2 * ML_V), F32),
                   jax.ShapeDtypeStruct((bsz, 2 * ML_HEADS, LANES), F32)],
        scratch_shapes=[pltpu.VMEM((ML_HEADS, ML_QK, 2 * ML_V), F32), pltpu.VMEM((2 * ML_HEADS, LANES), F32)],
        compiler_params=_cparams(("parallel", "arbitrary")),
        name="mlstm_prompt",
    )(mq, kT, mv, mo, misc, grow, bcol, brow, g_mlstm.reshape(1, ML_W))
    c_state = s_out[:, :, :, :ML_V].swapaxes(2, 3)
    n_state = s_out[:, :, :, ML_V]
    m_state = m_out[:, :ML_HEADS, 0]
    return hm, c_state, n_state, m_state


def _outproj_kernel(x_ref, att_ref, hm_ref, wo_ref, g_ref, wr_ref, br_ref, x1_ref, h2_ref, ridx_ref, rgate_ref):
    x1 = (x_ref[...]
          + jnp.dot(att_ref[...], wo_ref[:ATT_W, :], preferred_element_type=F32)
          + jnp.dot(hm_ref[...], wo_ref[ATT_W:, :], preferred_element_type=F32))
    x1_ref[...] = x1
    r = lax.rsqrt(jnp.mean(x1 * x1, axis=-1, keepdims=True) + EPS)
    h2 = x1 * r * g_ref[...]
    h2_ref[...] = h2
    logits = _dot_f32(h2, wr_ref[...]) + br_ref[...]
    lane = lax.broadcasted_iota(I32, logits.shape, 1)
    ridx = jnp.zeros(logits.shape, I32)
    vals = []
    for kk in range(TOP_K):
        mx = jnp.max(logits, axis=1, keepdims=True)
        am = jnp.min(jnp.where(logits == mx, lane, LANES), axis=1, keepdims=True)
        ridx = jnp.where(lane == kk, am, ridx)
        vals.append(mx)
        logits = jnp.where(lane == am, -jnp.inf, logits)
    es = [jnp.exp(v - vals[0]) for v in vals]
    tot = es[0] + es[1] + es[2] + es[3]
    rgate = jnp.zeros(logits.shape, F32)
    for kk in range(TOP_K):
        rgate = jnp.where(lane == kk, es[kk] / tot, rgate)
    ridx_ref[...] = ridx
    rgate_ref[...] = rgate


def _outproj(x2, att, hm, wo, g_ffn, wr, br, tm):
    n, d = x2.shape
    assert n % tm == 0
    row = lambda w: pl.BlockSpec((tm, w), lambda i: (i, 0))
    cst = lambda s: pl.BlockSpec(s, lambda i: (0, 0))
    return pl.pallas_call(
        _outproj_kernel,
        grid=(n // tm,),
        in_specs=[row(d), row(ATT_W), row(ML_W), cst((ATT_W + ML_W, d)), cst((1, d)), cst((d, LANES)), cst((1, LANES))],
        out_specs=[row(d), row(d), row(LANES), row(LANES)],
        out_shape=[jax.ShapeDtypeStruct((n, d), F32), jax.ShapeDtypeStruct((n, d), F32),
                   jax.ShapeDtypeStruct((n, LANES), I32), jax.ShapeDtypeStruct((n, LANES), F32)],
        compiler_params=_cparams(("parallel",)),
        name="outproj_router",
    )(x2, att, hm, wo, g_ffn.reshape(1, d), wr, br)


def _router_params(w_router, b_router):
    d = w_router.shape[0]
    wr = jnp.zeros((d, LANES), F32).at[:, :N_EXPERTS].set(w_router.astype(F32))
    br = jnp.full((1, LANES), -jnp.inf, F32).at[0, :N_EXPERTS].set(b_router.astype(F32))
    return wr, br


def _route(top_idx, block):
    n_tok = top_idx.shape[0]
    n_asg = n_tok * TOP_K
    assert n_asg % block == 0
    n_blocks = n_asg // block
    e_flat = top_idx.reshape(n_asg).astype(I32)
    order = jnp.argsort(e_flat).astype(I32)
    slot_tok = order // TOP_K
    slot_dst = (order % TOP_K) * n_tok + slot_tok
    counts = jnp.bincount(e_flat, length=N_EXPERTS).astype(I32)
    ends = jnp.cumsum(counts).astype(I32)
    starts = ends - counts
    blk_lo = jnp.arange(n_blocks, dtype=I32) * block
    blk_hi = blk_lo + block
    e_first = jnp.sum(ends[None, :] <= blk_lo[:, None], axis=1).astype(I32)
    e_last = jnp.sum(starts[None, :] < blk_hi[:, None], axis=1).astype(I32) - 1
    n_per = e_last - e_first + 1
    item_start = jnp.cumsum(n_per).astype(I32) - n_per
    n_items = n_blocks + N_EXPERTS - 1
    it = jnp.arange(n_items, dtype=I32)
    blk = jnp.sum(item_start[None, :] <= it[:, None], axis=1).astype(I32) - 1
    k_in = it - item_start[blk]
    real = k_in < n_per[blk]
    exp = jnp.minimum(e_first[blk] + k_in, N_EXPERTS - 1)
    lo = jnp.where(real, jnp.clip(starts[exp], blk_lo[blk], blk_hi[blk]) - blk_lo[blk], 0)
    hi = jnp.where(real, jnp.clip(ends[exp], blk_lo[blk], blk_hi[blk]) - blk_lo[blk], 0)
    exp = jnp.where(real, exp, e_last[n_blocks - 1])
    items = (blk, exp.astype(I32), lo.astype(I32), hi.astype(I32), (k_in == 0).astype(I32))
    return slot_tok.reshape(n_blocks, 1, block), slot_dst.reshape(n_blocks, 1, block), items


def _moe_kernel(blk_ref, exp_ref, lo_ref, hi_ref, first_ref, tok_ref, tokn_ref, dst_ref, x_hbm,
                wg_ref, bg_ref, wu_ref, bu_ref, wd_ref, bd_ref, ys_hbm, xbuf, ybuf, xb_ref, act_ref, gsem, ssem, *, n_blocks):
    i = pl.program_id(0)
    n_items = pl.num_programs(0)
    bs = xbuf.shape[1]
    b = blk_ref[i]
    lo, hi = lo_ref[i], hi_ref[i]
    first = first_ref[i] > 0
    xs = b % 2
    ysl = i % 2

    prefetch = first & (b + 1 < n_blocks)
    busy = hi > lo

    def gather_row(tref, s, r):
        pltpu.make_async_copy(x_hbm.at[pl.ds(tref[0, r], 1)], xbuf.at[s, pl.ds(r, 1)], gsem.at[s]).start()

    def gather_rows(tref, s, r0, r1):
        for r in range(r0, r1):
            gather_row(tref, s, r)

    def gather(tref, s):
        def body(r, _):
            gather_row(tref, s, r)
            return 0
        lax.fori_loop(0, bs, body, 0, unroll=8)

    def wait_scatter(s, n):
        p = bs
        while p >= 1:
            @pl.when((n & p) != 0)
            def _(p=p):
                pltpu.make_async_copy(ybuf.at[s, pl.ds(0, p)], ys_hbm.at[pl.ds(0, p)], ssem.at[s]).wait()
            p //= 2

    @pl.when(i == 0)
    def _first_block():
        gather(tok_ref, 0)

    @pl.when(prefetch & jnp.logical_not(busy))
    def _prefetch_next_block():
        gather(tokn_ref, 1 - xs)

    @pl.when(first)
    def _rows_ready():
        pltpu.make_async_copy(x_hbm.at[pl.ds(0, bs)], xbuf.at[xs], gsem.at[xs]).wait()

    @pl.when(i >= 2)
    def _result_buffer_free():
        wait_scatter(ysl, hi_ref[i - 2] - lo_ref[i - 2])

    def expert(prefetch_rows):
        xb_ref[...] = xbuf[xs].astype(BF16)
        dff = wg_ref.shape[1]
        tiles = dff // MOE_COL_TILE
        per = bs // (2 * tiles)
        seg = 0
        for t in range(tiles):
            cs = slice(t * MOE_COL_TILE, (t + 1) * MOE_COL_TILE)
            g = jnp.minimum(jnp.dot(xb_ref[...], wg_ref[:, cs], preferred_element_type=F32) + bg_ref[:, cs], SWIGLU_LIMIT)
            if prefetch_rows:
                gather_rows(tokn_ref, 1 - xs, seg * per, (seg + 1) * per)
                seg += 1
            u = jnp.clip(jnp.dot(xb_ref[...], wu_ref[:, cs], preferred_element_type=F32) + bu_ref[:, cs],
                         -SWIGLU_LIMIT, SWIGLU_LIMIT)
            if prefetch_rows:
                gather_rows(tokn_ref, 1 - xs, seg * per, (seg + 1) * per if seg + 1 < 2 * tiles else bs)
                seg += 1
            act_ref[:, cs] = ((u + 1.0) * (g * jax.nn.sigmoid(SWIGLU_ALPHA * g))).astype(BF16)
        ybuf[ysl] = jnp.dot(act_ref[...], wd_ref[...], preferred_element_type=F32) + bd_ref[...]

        def scatter(r, _):
            pltpu.make_async_copy(ybuf.at[ysl, pl.ds(r, 1)], ys_hbm.at[pl.ds(dst_ref[0, r], 1)], ssem.at[ysl]).start()
            return 0
        lax.fori_loop(lo, hi, scatter, 0)

    @pl.when(busy & prefetch)
    def _compute_and_prefetch():
        expert(True)

    @pl.when(busy & jnp.logical_not(prefetch))
    def _compute():
        expert(False)

    @pl.when(i == n_items - 1)
    def _drain():
        wait_scatter(ysl, hi - lo)

        @pl.when(i >= 1)
        def _():
            wait_scatter(1 - ysl, hi_ref[i - 1] - lo_ref[i - 1])


def _moe(h2, top_idx, wg, bg, wu, bu, wd, bd, block):
    n, d = h2.shape
    dff = wg.shape[2]
    slot_tok, slot_dst, items = _route(top_idx, block)
    n_blocks = slot_tok.shape[0]
    n_items = items[0].shape[0]
    smem_blk = lambda f: pl.BlockSpec((None, 1, block), f, memory_space=pltpu.SMEM)
    wspec = lambda a, b_: pl.BlockSpec((None, a, b_), lambda i, blk, exp, *_: (exp[i], 0, 0))
    grid_spec = pltpu.PrefetchScalarGridSpec(
        num_scalar_prefetch=5,
        grid=(n_items,),
        in_specs=[smem_blk(lambda i, blk, *_: (blk[i], 0, 0)),
                  smem_blk(lambda i, blk, *_: (jnp.minimum(blk[i] + 1, n_blocks - 1), 0, 0)),
                  smem_blk(lambda i, blk, *_: (blk[i], 0, 0)),
                  pl.BlockSpec(memory_space=pl.ANY),
                  wspec(d, dff), wspec(1, dff), wspec(d, dff), wspec(1, dff), wspec(dff, d), wspec(1, d)],
        out_specs=pl.BlockSpec(memory_space=pl.ANY),
        scratch_shapes=[pltpu.VMEM((2, block, d), F32), pltpu.VMEM((2, block, d), F32),
                        pltpu.VMEM((block, d), BF16), pltpu.VMEM((block, dff), BF16),
                        pltpu.SemaphoreType.DMA((2,)), pltpu.SemaphoreType.DMA((2,))],
    )
    assert dff % MOE_COL_TILE == 0 and block % (2 * (dff // MOE_COL_TILE)) == 0
    return pl.pallas_call(
        functools.partial(_moe_kernel, n_blocks=n_blocks),
        grid_spec=grid_spec,
        out_shape=jax.ShapeDtypeStruct((n * TOP_K, d), F32),
        compiler_params=_cparams(("arbitrary",)),
        name="moe_experts",
    )(*items, slot_tok, slot_tok, slot_dst, h2,
      wg, bg.reshape(N_EXPERTS, 1, dff), wu, bu.reshape(N_EXPERTS, 1, dff), wd, bd.reshape(N_EXPERTS, 1, d))


def _combine_kernel(x1_ref, y0_ref, y1_ref, y2_ref, y3_ref, gate_ref, g_ref, y_ref):
    x = x1_ref[...]
    for kk, ys_ref in enumerate((y0_ref, y1_ref, y2_ref, y3_ref)):
        x = x + gate_ref[:, kk:kk + 1] * ys_ref[...]
    r = lax.rsqrt(jnp.mean(x * x, axis=-1, keepdims=True) + EPS)
    y_ref[...] = x * r * g_ref[...]


def _combine(x1, ys, rgate, g_final, tm):
    n, d = x1.shape
    nt = n // tm
    row = lambda w: pl.BlockSpec((tm, w), lambda i: (i, 0))
    ys_specs = [pl.BlockSpec((tm, d), lambda i, j=j: (j * nt + i, 0)) for j in range(TOP_K)]
    return pl.pallas_call(
        _combine_kernel,
        grid=(nt,),
        in_specs=[row(d)] + ys_specs + [row(LANES), pl.BlockSpec((1, d), lambda i: (0, 0))],
        out_specs=row(d),
        out_shape=jax.ShapeDtypeStruct((n, d), F32),
        compiler_params=_cparams(("parallel",)),
        name="combine_norm",
    )(x1, ys, ys, ys, ys, rgate, g_final.reshape(1, d))


def _page_copies(pt_ref, b, cache_hbm, buf, sem, first_page, n_pages):
    return [pltpu.make_async_copy(cache_hbm.at[pt_ref[b, first_page + p]], buf.at[p], sem) for p in range(n_pages)]


def _dsa_sample_select_kernel(pt_ref, iq_ref, iw_ref, ikown_ref, cache_hbm, bias_ref, kbuf, sem, *, topk, page):
    b = pl.program_id(0)
    nb = pl.num_programs(0)
    past = kbuf.shape[2]
    n_pages = past // page
    nk = bias_ref.shape[1]
    slot = b % 2

    def copies(bb, s):
        return [pltpu.make_async_copy(cache_hbm.at[pt_ref[bb, p]], kbuf.at[s, :, p * page:(p + 1) * page], sem.at[s])
                for p in range(n_pages)]

    @pl.when(b == 0)
    def _first():
        for cp in copies(0, 0):
            cp.start()

    @pl.when(b + 1 < nb)
    def _prefetch():
        for cp in copies(b + 1, 1 - slot):
            cp.start()

    for cp in copies(b, slot):
        cp.wait()

    iq = iq_ref[...]
    w = iw_ref[...]
    s = jnp.dot(iq, kbuf[slot].astype(BF16), preferred_element_type=F32)
    score = jnp.sum(w * jnp.maximum(s, 0.0), axis=0, keepdims=True)
    own = ikown_ref[...].astype(BF16).astype(F32)
    s_own = jnp.sum(iq.astype(F32) * own, axis=1, keepdims=True)
    score_own = jnp.sum(w * jnp.maximum(s_own, 0.0), axis=0, keepdims=True)
    tail_lane = lax.broadcasted_iota(I32, (1, nk - past), 1)
    key = jnp.concatenate([_sort_key(score),
                           jnp.where(tail_lane == 0, _sort_key(score_own), KEY_MASKED)], axis=1)
    idx = lax.broadcasted_iota(I32, (1, nk), 1)

    def count_ge(cand):
        return jnp.sum((key >= cand).astype(I32), axis=1, keepdims=True)

    t_k, _ = _kth_largest_key(count_ge, nk, topk, (1, 1))
    need = topk - count_ge(t_k + 1)
    eq = key == t_k
    nbits = int(nk).bit_length()

    def idx_step(i, jb):
        cand = jb + (jnp.int32(1) << (nbits - 1 - i))
        c = jnp.sum((eq & (idx < cand)).astype(I32), axis=1, keepdims=True)
        return jnp.where(c <= need, cand, jb)

    j_star = lax.fori_loop(0, nbits, idx_step, jnp.zeros((1, 1), I32))
    sel = ((key > t_k) | (eq & (idx < j_star))) & (key > KEY_NEG_INF)
    bias_ref[...] = jnp.where(sel, 0.0, -jnp.inf).astype(F32)


def _dsa_sample_attend_kernel(pt_ref, q_ref, kown_ref, vown_ref, bias_ref, bias_own_ref, ck_hbm, cv_hbm, o_ref,
                              kbuf, vbuf, ksem, vsem, acc_ref, m_ref, l_ref):
    b = pl.program_id(0)
    c = pl.program_id(1)
    nb, ncg = pl.num_programs(0), pl.num_programs(1)
    pg, page = kbuf.shape[1], kbuf.shape[4]
    step = b * ncg + c
    slot = step % 2

    def copies(bb, cc, s):
        return (_page_copies(pt_ref, bb, ck_hbm, kbuf.at[s], ksem.at[s], cc * pg, pg)
                + _page_copies(pt_ref, bb, cv_hbm, vbuf.at[s], vsem.at[s], cc * pg, pg))

    @pl.when(step == 0)
    def _first():
        for cp in copies(0, 0, 0):
            cp.start()

    @pl.when(step + 1 < nb * ncg)
    def _prefetch():
        nxt = step + 1
        for cp in copies(nxt // ncg, nxt % ncg, 1 - slot):
            cp.start()

    @pl.when(c == 0)
    def _init():
        acc_ref[...] = jnp.zeros_like(acc_ref)
        m_ref[...] = jnp.full(m_ref.shape, -1e30, F32)
        l_ref[...] = jnp.zeros_like(l_ref)

    for cp in copies(b, c, slot):
        cp.wait()

    for h in range(ATT_HEADS):
        qc = q_ref[h]
        s = [jnp.sum(kbuf[slot, g, h] * qc, axis=0, keepdims=True) + bias_ref[:, g * page:(g + 1) * page]
             for g in range(pg)]
        top = functools.reduce(jnp.maximum, s)
        m_old = m_ref[h]
        m_new = jnp.maximum(m_old, jnp.max(top, axis=1, keepdims=True))
        alpha = jnp.exp2(m_old - m_new)
        acc = alpha * acc_ref[h]
        lsum = alpha * l_ref[h]
        for g in range(pg):
            p = jnp.exp2(s[g] - m_new)
            lsum = lsum + p
            acc = acc + p * vbuf[slot, g, h]
        acc_ref[h] = acc
        l_ref[h] = lsum
        m_ref[h] = m_new

    @pl.when(c == ncg - 1)
    def _finish():
        for h in range(ATT_HEADS):
            s_own = jnp.sum(q_ref[h] * kown_ref[h], axis=0, keepdims=True) + bias_own_ref[:, 0:1]
            m_old = m_ref[h]
            m_new = jnp.maximum(m_old, s_own)
            alpha = jnp.exp2(m_old - m_new)
            p_own = jnp.exp2(s_own - m_new)[:, 0:1]
            l_tot = jnp.sum(alpha * l_ref[h], axis=1, keepdims=True) + p_own
            o = jnp.sum(alpha * acc_ref[h], axis=1, keepdims=True) + p_own * vown_ref[h]
            o_ref[h] = o / l_tot


def _dsa_sample(q, k_own, v_own, iq, ik_own, iw, cache_k, cache_v, cache_ik, page_table, pages_per_step):
    bsz = q.shape[0]
    n_pool, page = cache_ik.shape[:2]
    n_pages = page_table.shape[1]
    past = n_pages * page
    topk = min(TOPK_MAX, (past + 1) // 4)
    nk = past + LANES
    assert n_pages % pages_per_step == 0
    blk3 = lambda s: pl.BlockSpec((None,) + s, lambda b, *_: (b, 0, 0))
    blk4 = lambda s: pl.BlockSpec((None,) + s, lambda b, *_: (b, 0, 0, 0))
    bias = pl.pallas_call(
        functools.partial(_dsa_sample_select_kernel, topk=topk, page=page),
        grid_spec=pltpu.PrefetchScalarGridSpec(
            num_scalar_prefetch=1, grid=(bsz,),
            in_specs=[blk3((IDX_HEADS, IDX_DIM)), blk3((IDX_HEADS, 1)), blk3((1, IDX_DIM)),
                      pl.BlockSpec(memory_space=pl.ANY)],
            out_specs=blk3((1, nk)),
            scratch_shapes=[pltpu.VMEM((2, IDX_DIM, past), F32), pltpu.SemaphoreType.DMA((2,))]),
        out_shape=jax.ShapeDtypeStruct((bsz, 1, nk), F32),
        compiler_params=_cparams(("arbitrary",)),
        name="dsa_sample_select",
    )(page_table, iq.reshape(bsz, IDX_HEADS, IDX_DIM), iw.reshape(bsz, IDX_HEADS, 1), ik_own.reshape(bsz, 1, IDX_DIM),
      jnp.transpose(cache_ik, (0, 2, 1)))
    ncg = n_pages // pages_per_step
    keys_per_step = pages_per_step * page
    bias_past = bias[:, :, :past].reshape(bsz, ncg, 1, keys_per_step)
    bias_own = bias[:, :, past:]
    col = lambda a: a.astype(F32).reshape(bsz, ATT_HEADS, ATT_DIM, 1)
    pages = pltpu.VMEM((2, pages_per_step, ATT_HEADS, ATT_DIM, page), F32)
    att = pl.pallas_call(
        _dsa_sample_attend_kernel,
        grid_spec=pltpu.PrefetchScalarGridSpec(
            num_scalar_prefetch=1, grid=(bsz, ncg),
            in_specs=[blk4((ATT_HEADS, ATT_DIM, 1)), blk4((ATT_HEADS, ATT_DIM, 1)), blk4((ATT_HEADS, ATT_DIM, 1)),
                      pl.BlockSpec((None, None, 1, keys_per_step), lambda b, c, *_: (b, c, 0, 0)), blk3((1, LANES)),
                      pl.BlockSpec(memory_space=pl.ANY), pl.BlockSpec(memory_space=pl.ANY)],
            out_specs=blk4((ATT_HEADS, ATT_DIM, 1)),
            scratch_shapes=[pages, pages, pltpu.SemaphoreType.DMA((2,)), pltpu.SemaphoreType.DMA((2,)),
                            pltpu.VMEM((ATT_HEADS, ATT_DIM, page), F32), pltpu.VMEM((ATT_HEADS, 1, page), F32),
                            pltpu.VMEM((ATT_HEADS, 1, page), F32)]),
        out_shape=jax.ShapeDtypeStruct((bsz, ATT_HEADS, ATT_DIM, 1), F32),
        compiler_params=_cparams(("arbitrary", "arbitrary")),
        name="dsa_sample_attend",
    )(page_table, col(q), col(k_own), col(v_own), bias_past, bias_own,
      jnp.transpose(cache_k, (0, 2, 3, 1)), jnp.transpose(cache_v, (0, 2, 3, 1)))
    return att.reshape(bsz, ATT_W).astype(BF16)


def _mlstm_step_kernel(q_ref, k_ref, v_ref, mo_ref, g_ref, gate_ref, gbias_ref, c_ref, n_ref, m_ref,
                       hm_ref, c_out_ref, n_out_ref, m_out_ref):
    gates = _softcap(gate_ref[...] + gbias_ref[...])
    lane = lax.broadcasted_iota(I32, (1, LANES), 1)
    m_out = jnp.zeros((1, LANES), F32)
    for h in range(ML_HEADS):
        ig = gates[:, M_MI + h:M_MI + h + 1]
        lf = _log_sigmoid(gates[:, M_MF + h:M_MF + h + 1])
        m_prev = m_ref[:, h:h + 1]
        q = q_ref[:, h * ML_QK:(h + 1) * ML_QK].astype(F32)
        k = k_ref[:, h * ML_QK:(h + 1) * ML_QK].astype(F32)
        v = v_ref[h].astype(F32)
        c = c_ref[h]
        n = n_ref[h:h + 1, :]
        m_inter = lf + m_prev
        m_t = jnp.maximum(m_inter, ig)
        w_intra = jnp.exp(ig - m_t)
        w_inter = jnp.exp(m_inter - m_t)
        qk = jnp.sum(q * k, axis=1, keepdims=True) * w_intra
        num = w_inter * jnp.sum(c * q, axis=1, keepdims=True) + qk * v
        den = w_inter * jnp.sum(n * q, axis=1, keepdims=True) + qk
        hh = num / jnp.maximum(jnp.abs(den), jnp.exp(-m_t))
        hh = hh * lax.rsqrt(jnp.mean(hh * hh, axis=0, keepdims=True) + EPS)
        hm_ref[h] = (hh * g_ref[h] * jax.nn.sigmoid(mo_ref[h])).astype(hm_ref.dtype)
        decay = jnp.exp(lf + m_prev - m_t)
        w_s = jnp.exp(ig - m_t)
        c_out_ref[h] = decay * c + (w_s * v) * k
        n_out_ref[h:h + 1, :] = decay * n + w_s * k
        m_out = jnp.where(lane == h, m_t, m_out)
    m_out_ref[...] = m_out


def _mlstm_step(mq, mk, mv, mo, misc, b_igate, b_fgate, g_mlstm, state_c, state_n, state_m):
    bsz = mq.shape[0]
    _, brow = _gate_bias(b_igate, b_fgate)
    b3 = lambda s: pl.BlockSpec((None,) + s, lambda b: (b,) + (0,) * len(s))
    cst = lambda s: pl.BlockSpec(s, lambda b: (0,) * len(s))
    hm, c_new, n_new, m_new = pl.pallas_call(
        _mlstm_step_kernel,
        grid=(bsz,),
        in_specs=[b3((1, ML_QW)), b3((1, ML_QW)), b3((ML_HEADS, ML_V, 1)), b3((ML_HEADS, ML_V, 1)), cst((ML_HEADS, ML_V, 1)),
                  b3((1, LANES)), cst((1, LANES)), b3((ML_HEADS, ML_V, ML_QK)), b3((ML_HEADS, ML_QK)), b3((1, ML_HEADS))],
        out_specs=[b3((ML_HEADS, ML_V, 1)), b3((ML_HEADS, ML_V, ML_QK)), b3((ML_HEADS, ML_QK)), b3((1, LANES))],
        out_shape=[jax.ShapeDtypeStruct((bsz, ML_HEADS, ML_V, 1), F32),
                   jax.ShapeDtypeStruct((bsz, ML_HEADS, ML_V, ML_QK), F32),
                   jax.ShapeDtypeStruct((bsz, ML_HEADS, ML_QK), F32),
                   jax.ShapeDtypeStruct((bsz, 1, LANES), F32)],
        compiler_params=_cparams(("parallel",)),
        name="mlstm_step",
    )(mq.reshape(bsz, 1, ML_QW), mk.reshape(bsz, 1, ML_QW), mv.astype(F32).reshape(bsz, ML_HEADS, ML_V, 1),
      mo.reshape(bsz, ML_HEADS, ML_V, 1), g_mlstm.astype(F32).reshape(ML_HEADS, ML_V, 1),
      misc.reshape(bsz, 1, LANES), brow, state_c, state_n, state_m.reshape(bsz, 1, ML_HEADS))
    return hm.reshape(bsz, ML_W).astype(BF16), c_new, n_new, m_new[:, 0, :ML_HEADS]


TQ = 512
TM_PROJ = 512
ML_CHUNK = 256
MOE_BLOCK_PROMPT = 512
MOE_BLOCK_SAMPLE = 128
MOE_COL_TILE = 256
SAMPLE_PAGES_PER_STEP = 8


def kernel(x_prompt, x_sample, cache_k, cache_v, cache_idx_k, state_C, state_n, state_m, page_table, g_mix, w_in, b_igate, b_fgate, g_mlstm, w_out, g_ffn, w_router, b_router, w_gate, b_gate, w_up, b_up, w_down, b_down, g_final):
    bp, tp, d = x_prompt.shape
    bs, ts, _ = x_sample.shape
    assert w_in.shape[0] == 1 and ts == 1, "one layer, one new token per sampled sequence"
    page = cache_k.shape[2]
    n_pages = page_table.shape[1]
    past = n_pages * page

    w = _build_w_in(w_in[0])
    wo = w_out[0].astype(BF16)
    wr, br = _router_params(w_router[0], b_router[0])
    moe_w = (w_gate[0].astype(BF16), b_gate[0], w_up[0].astype(BF16), b_up[0], w_down[0].astype(BF16), b_down[0])

    def tail(x2, att, hm, tm, moe_block):
        x1, h2, ridx, rgate = _outproj(x2, att, hm, wo, g_ffn[0], wr, br, tm)
        ys = _moe(h2, ridx[:, :TOP_K], *moe_w, moe_block)
        return _combine(x1, ys, rgate, g_final, tm)

    xp2 = x_prompt.reshape(bp * tp, d)
    cos_p, sin_p = _rope_tables(jnp.arange(tp))
    q, kf, kb, vf, vb, iq, misc, mq, mk, mv, mo = _inproj(xp2, g_mix[0], w, cos_p, sin_p, TM_PROJ)
    ik_p = misc[:, M_IK:M_IK + IDX_DIM]
    att = _dsa_prompt(q, kb, vb, iq, ik_p.astype(BF16), misc[:, M_IW:M_IW + IDX_HEADS], bp, tp, TQ)
    hm, c_p, n_p, m_p = _mlstm_prompt(mq, mk, mv, mo, misc, b_igate[0], b_fgate[0], g_mlstm[0], bp, tp, ML_CHUNK)
    y_prompt = tail(xp2, att, hm, TM_PROJ, MOE_BLOCK_PROMPT).reshape(bp, tp, d)

    xs2 = x_sample.reshape(bs, d)
    cos_s, sin_s = _rope_tables(jnp.full((bs,), past))
    q_s, kf_s, kb_s, vf_s, vb_s, iq_s, misc_s, mq_s, mk_s, mv_s, mo_s = _inproj(xs2, g_mix[0], w, cos_s, sin_s, bs)
    ik_s = misc_s[:, M_IK:M_IK + IDX_DIM]
    att_s = _dsa_sample(q_s, kf_s, vf_s, iq_s, ik_s, misc_s[:, M_IW:M_IW + IDX_HEADS],
                        cache_k[0], cache_v[0], cache_idx_k[0], page_table, math.gcd(SAMPLE_PAGES_PER_STEP, n_pages))
    hm_s, c_s, n_s, m_s = _mlstm_step(mq_s, mk_s, mv_s, mo_s, misc_s, b_igate[0], b_fgate[0], g_mlstm[0],
                                      state_C[0], state_n[0], state_m[0])
    y_sample = tail(xs2, att_s, hm_s, bs, min(MOE_BLOCK_SAMPLE, bs * TOP_K)).reshape(bs, ts, d)

    return (y_prompt, y_sample,
            kf.reshape(1, bp, tp // page, page, ATT_HEADS, ATT_DIM), vf.reshape(1, bp, tp // page, page, ATT_HEADS, ATT_DIM),
            ik_p.reshape(1, bp, tp // page, page, IDX_DIM), c_p[None], n_p[None], m_p[None],
            kf_s.reshape(1, bs, ts, ATT_HEADS, ATT_DIM), vf_s.reshape(1, bs, ts, ATT_HEADS, ATT_DIM),
            ik_s.reshape(1, bs, ts, IDX_DIM), c_s[None], n_s[None], m_s[None])
```
